```python
import math
import jax, jax.numpy as jnp
from jax import lax
import numpy as np

D_MODEL = 1024
BATCH = 2
SEQ = 16384
DEPTH = 2

ATTN_WIDTH = D_MODEL // 2
HEAD_DIM = 64
V_DIM = 2 * HEAD_DIM
N_HEADS = ATTN_WIDTH // V_DIM
Q_BLOCK = 128
CONV_WIDTH = D_MODEL // 2
CONV_KERNEL = 31
N_BRANCHES = 2
EPS = 1e-6

COL_Q = N_HEADS * 2 * HEAD_DIM
COL_K = N_HEADS * 2 * HEAD_DIM
COL_V = ATTN_WIDTH
COL_AG = ATTN_WIDTH
COL_GLU = 2 * CONV_WIDTH
COL_CG = CONV_WIDTH
COL_MG = N_BRANCHES * D_MODEL
SPLITS = tuple(np.cumsum([COL_Q, COL_K, COL_V, COL_AG, COL_GLU, COL_CG])[:].tolist())
D_IN = COL_Q + COL_K + COL_V + COL_AG + COL_GLU + COL_CG + COL_MG

ALIBI_SLOPES = tuple(2.0 ** (-8.0 * (i + 1) / N_HEADS) for i in range(N_HEADS))

kernel_name = "hybrid_diffattn_conformer_gated_encoder"


def _rmsnorm(x, g):
    xf = x.astype(jnp.float32)
    y = xf * lax.rsqrt(jnp.mean(xf * xf, axis=-1, keepdims=True) + EPS)
    return (y * g.astype(jnp.float32)).astype(x.dtype)


def _layernorm(x, g, b):
    xf = x.astype(jnp.float32)
    mu = jnp.mean(xf, axis=-1, keepdims=True)
    xc = xf - mu
    y = xc * lax.rsqrt(jnp.mean(xc * xc, axis=-1, keepdims=True) + EPS)
    return (y * g.astype(jnp.float32) + b.astype(jnp.float32)).astype(x.dtype)


def _lambda_init(layer_idx):
    return 0.8 - 0.6 * math.exp(-0.3 * layer_idx)


def _diff_attention(q1, q2, k1, k2, v, lam):
    B, H, S, d = q1.shape
    nb = S // Q_BLOCK
    scale = 1.0 / math.sqrt(d)
    slopes = jnp.asarray(ALIBI_SLOPES, dtype=jnp.float32)
    kpos = jnp.arange(S, dtype=jnp.float32)
    lam = lam.astype(jnp.float32)

    def to_blocks(t):
        return t.reshape(B, H, nb, Q_BLOCK, d).transpose(2, 0, 1, 3, 4)

    def block(args):
        qa, qb, start = args
        qpos = (start + jnp.arange(Q_BLOCK, dtype=jnp.int32)).astype(jnp.float32)
        bias = -slopes[:, None, None] * jnp.abs(qpos[:, None] - kpos[None, :])
        s1 = jnp.einsum('bhqd,bhkd->bhqk', qa, k1).astype(jnp.float32) * scale + bias
        s2 = jnp.einsum('bhqd,bhkd->bhqk', qb, k2).astype(jnp.float32) * scale + bias
        a = (jax.nn.softmax(s1, axis=-1) - lam * jax.nn.softmax(s2, axis=-1)).astype(v.dtype)
        return jnp.einsum('bhqk,bhkv->bhqv', a, v)

    starts = jnp.arange(nb, dtype=jnp.int32) * Q_BLOCK
    out = lax.map(block, (to_blocks(q1), to_blocks(q2), starts))
    return out.transpose(1, 0, 3, 2, 4).reshape(B, S, H, v.shape[-1])


def _depthwise_conv(u, w, b):
    C = u.shape[-1]
    pad = (CONV_KERNEL - 1) // 2
    y = lax.conv_general_dilated(u, w.astype(u.dtype)[:, None, :], window_strides=(1,),
                                 padding=[(pad, pad)], dimension_numbers=('NWC', 'WIO', 'NWC'),
                                 feature_group_count=C)
    return y + b.astype(u.dtype)


def setup_inputs(seed: int = 0) -> dict:
    key = jax.random.key(seed)
    ks = jax.random.split(key, 20)
    f32 = jnp.float32
    nrm = lambda k, shape, s: jax.random.normal(k, shape, f32) * s
    return {
        "x": jax.random.normal(ks[0], (BATCH, SEQ, D_MODEL), f32),
        "g_pre": 1.0 + nrm(ks[1], (DEPTH, D_MODEL), 0.02),
        "w_in": nrm(ks[2], (DEPTH, D_MODEL, D_IN), D_MODEL ** -0.5),
        "g_q": 1.0 + nrm(ks[3], (DEPTH, HEAD_DIM), 0.02),
        "g_k": 1.0 + nrm(ks[4], (DEPTH, HEAD_DIM), 0.02),
        "lam_q1": nrm(ks[5], (DEPTH, HEAD_DIM), 0.1),
        "lam_k1": nrm(ks[6], (DEPTH, HEAD_DIM), 0.1),
        "lam_q2": nrm(ks[7], (DEPTH, HEAD_DIM), 0.1),
        "lam_k2": nrm(ks[8], (DEPTH, HEAD_DIM), 0.1),
        "g_subln": 1.0 + nrm(ks[9], (DEPTH, V_DIM), 0.02),
        "w_attn_out": nrm(ks[10], (DEPTH, ATTN_WIDTH, D_MODEL), ATTN_WIDTH ** -0.5),
        "w_dw": nrm(ks[11], (DEPTH, CONV_KERNEL, CONV_WIDTH), CONV_KERNEL ** -0.5),
        "b_dw": nrm(ks[12], (DEPTH, CONV_WIDTH), 0.02),
        "g_cn": 1.0 + nrm(ks[13], (DEPTH, CONV_WIDTH), 0.02),
        "b_cn": nrm(ks[14], (DEPTH, CONV_WIDTH), 0.02),
        "w_conv_out": nrm(ks[15], (DEPTH, CONV_WIDTH, D_MODEL), CONV_WIDTH ** -0.5),
        "w_out": nrm(ks[16], (DEPTH, D_MODEL, D_MODEL), D_MODEL ** -0.5),
    }


def reference(x, g_pre, w_in, g_q, g_k, lam_q1, lam_k1, lam_q2, lam_k2, g_subln,
              w_attn_out, w_dw, b_dw, g_cn, b_cn, w_conv_out, w_out):
    B, S, _ = x.shape
    for l in range(DEPTH):
        h = _rmsnorm(x, g_pre[l])
        z = h @ w_in[l].astype(h.dtype)
        zq, zk, zv, zag, zglu, zcg, zmg = jnp.split(z, SPLITS, axis=-1)

        q = _rmsnorm(zq.reshape(B, S, N_HEADS, 2, HEAD_DIM), g_q[l])
        k = _rmsnorm(zk.reshape(B, S, N_HEADS, 2, HEAD_DIM), g_k[l])
        q = q.transpose(3, 0, 2, 1, 4)
        k = k.transpose(3, 0, 2, 1, 4)
        v = zv.reshape(B, S, N_HEADS, V_DIM).transpose(0, 2, 1, 3)
        lam_init = _lambda_init(l)
        lam = (jnp.exp(jnp.sum(lam_q1[l].astype(jnp.float32) * lam_k1[l].astype(jnp.float32)))
               - jnp.exp(jnp.sum(lam_q2[l].astype(jnp.float32) * lam_k2[l].astype(jnp.float32)))
               + lam_init)
        o = _diff_attention(q[0], q[1], k[0], k[1], v, lam)
        o = (_rmsnorm(o, g_subln[l]) * (1.0 - lam_init)).reshape(B, S, ATTN_WIDTH)
        y_a = (o * jax.nn.silu(zag)) @ w_attn_out[l].astype(o.dtype)

        ua, ub = jnp.split(zglu, 2, axis=-1)
        u = ua * jax.nn.sigmoid(ub)
        u = _depthwise_conv(u, w_dw[l], b_dw[l])
        u = jax.nn.silu(_layernorm(u, g_cn[l], b_cn[l]))
        y_b = (u * jax.nn.silu(zcg)) @ w_conv_out[l].astype(u.dtype)

        ga, gb = jnp.split(jax.nn.sigmoid(zmg), N_BRANCHES, axis=-1)
        m = ga * y_a + gb * y_b
        x = x + m @ w_out[l].astype(m.dtype)
    return x
```

```python
import functools
import math

import numpy as np
import jax
import jax.numpy as jnp
from jax import lax
from jax.experimental import pallas as pl
from jax.experimental.pallas import tpu as pltpu

D_MODEL = 1024
HEAD_DIM = 64
V_DIM = 2 * HEAD_DIM
ATTN_WIDTH = D_MODEL // 2
N_HEADS = ATTN_WIDTH // V_DIM
CONV_WIDTH = D_MODEL // 2
CONV_KERNEL = 31
CONV_PAD = (CONV_KERNEL - 1) // 2
EPS = 1e-6
ALIBI_SLOPES = tuple(2.0 ** (-8.0 * (i + 1) / N_HEADS) for i in range(N_HEADS))
LOG2E = math.log2(math.e)

OFF_Q = 0
OFF_K = OFF_Q + N_HEADS * 2 * HEAD_DIM
OFF_V = OFF_K + N_HEADS * 2 * HEAD_DIM
OFF_AG = OFF_V + ATTN_WIDTH
OFF_GLU = OFF_AG + ATTN_WIDTH
OFF_CG = OFF_GLU + 2 * CONV_WIDTH
OFF_MG = OFF_CG + CONV_WIDTH
D_IN = OFF_MG + 2 * D_MODEL

LANES = 128
SUBLANES = 8
VMEM_LIMIT_BYTES = 56 * 1024 * 1024

TM_PROJ = 256
TM_MERGE = 256
TQ = 256
TK = 1024
HALO = 16
JL_BITS = 5

N_AUG = 9
NEG_INIT = -1e30

assert TK % TQ == 0 and (TK >> JL_BITS) <= 256 and TQ <= 256 and HALO >= CONV_PAD


def _bf16_pieces(v, n=3):
    out = []
    r = v
    for _ in range(n):
        p = r.astype(jnp.bfloat16).astype(jnp.float32)
        out.append(p)
        r = r - p
    return out


def _bf16_pieces_py(v, n=3):
    out = []
    r = float(v)
    for _ in range(n):
        p = float(np.asarray(r, np.float32).astype(jnp.bfloat16).astype(np.float32))
        out.append(p)
        r = r - p
    return out


def _silu(x):
    return x * jax.nn.sigmoid(x)


def _proj_kernel(x_ref, gpre_ref, w_ref, gq_ref, gk_ref,
                 qa_ref, ka_ref, vt_ref, sag_ref, u_ref, scg_ref):
    tm = x_ref.shape[1]
    x = x_ref[0]
    h = x * lax.rsqrt(jnp.mean(x * x, axis=-1, keepdims=True) + EPS) * gpre_ref[...]
    h = h.astype(jnp.bfloat16)

    def proj(lo, width):
        return jnp.dot(h, w_ref[:, lo:lo + width], preferred_element_type=jnp.float32)

    lane = lax.broadcasted_iota(jnp.int32, (tm, LANES), 1)
    row = pl.program_id(1) * tm + lax.broadcasted_iota(jnp.int32, (tm, LANES), 0)
    lo_half = lane < HEAD_DIM
    aug_idx = lane & (HEAD_DIM - 1)

    j_rel = row & (TK - 1)
    jh = (j_rel >> JL_BITS).astype(jnp.float32)
    jl = (j_rel & ((1 << JL_BITS) - 1)).astype(jnp.float32)
    aug_k = jnp.where(aug_idx < 3, jh,
                      jnp.where(aug_idx < 6, jl,
                                jnp.where(aug_idx < N_AUG, 1.0, 0.0)))
    i_rel = (row & (TQ - 1)).astype(jnp.float32)

    def qk_norm(z, g):
        zz = z * z
        s_lo = jnp.sum(jnp.where(lo_half, zz, 0.0), axis=-1, keepdims=True)
        s_hi = jnp.sum(jnp.where(lo_half, 0.0, zz), axis=-1, keepdims=True)
        r = jnp.where(lo_half, lax.rsqrt(s_lo * (1.0 / HEAD_DIM) + EPS),
                      lax.rsqrt(s_hi * (1.0 / HEAD_DIM) + EPS))
        return z * r * g

    zq = proj(OFF_Q, N_HEADS * 2 * HEAD_DIM)
    zk = proj(OFF_K, N_HEADS * 2 * HEAD_DIM)
    q_scale = LOG2E / math.sqrt(HEAD_DIM)
    for hh in range(N_HEADS):
        cs = ALIBI_SLOPES[hh] * LOG2E
        c = _bf16_pieces_py(cs)
        t = _bf16_pieces(-cs * i_rel)
        hi_w = float(1 << JL_BITS)
        aug_q = jnp.where(aug_idx == 0, hi_w * c[0], jnp.where(aug_idx == 1, hi_w * c[1],
                jnp.where(aug_idx == 2, hi_w * c[2], jnp.where(aug_idx == 3, c[0],
                jnp.where(aug_idx == 4, c[1], jnp.where(aug_idx == 5, c[2],
                jnp.where(aug_idx == 6, t[0], jnp.where(aug_idx == 7, t[1],
                jnp.where(aug_idx == 8, t[2], 0.0)))))))))
        sl = slice(hh * LANES, (hh + 1) * LANES)
        qn = qk_norm(zq[:, sl], gq_ref[...]) * q_scale
        kn = qk_norm(zk[:, sl], gk_ref[...])
        qa_ref[0, 2 * hh] = jnp.where(lo_half, qn, aug_q).astype(jnp.bfloat16)
        qa_ref[0, 2 * hh + 1] = jnp.where(lo_half, aug_q, qn).astype(jnp.bfloat16)
        ka_ref[0, 2 * hh] = jnp.where(lo_half, kn, aug_k).astype(jnp.bfloat16)
        ka_ref[0, 2 * hh + 1] = jnp.where(lo_half, aug_k, kn).astype(jnp.bfloat16)

    zv = proj(OFF_V, ATTN_WIDTH)
    for hh in range(N_HEADS):
        vt_ref[0, hh] = zv[:, hh * V_DIM:(hh + 1) * V_DIM].T.astype(jnp.bfloat16)

    sag_ref[0] = _silu(proj(OFF_AG, ATTN_WIDTH))
    zg = proj(OFF_GLU, 2 * CONV_WIDTH)
    u_ref[0] = zg[:, :CONV_WIDTH] * jax.nn.sigmoid(zg[:, CONV_WIDTH:])
    scg_ref[0] = _silu(proj(OFF_CG, CONV_WIDTH))


def _proj_call(x, g_pre, w_bf, g_q2, g_k2):
    B, S, _ = x.shape
    grid = (B, S // TM_PROJ)
    row_blk = lambda b, i: (b, i, 0)
    const2 = lambda b, i: (0, 0)
    out_shape = (
        jax.ShapeDtypeStruct((B, 2 * N_HEADS, S, LANES), jnp.bfloat16),
        jax.ShapeDtypeStruct((B, 2 * N_HEADS, S, LANES), jnp.bfloat16),
        jax.ShapeDtypeStruct((B, N_HEADS, V_DIM, S), jnp.bfloat16),
        jax.ShapeDtypeStruct((B, S, ATTN_WIDTH), jnp.float32),
        jax.ShapeDtypeStruct((B, S, CONV_WIDTH), jnp.float32),
        jax.ShapeDtypeStruct((B, S, CONV_WIDTH), jnp.float32),
    )
    return pl.pallas_call(
        _proj_kernel,
        grid=grid,
        in_specs=[
            pl.BlockSpec((1, TM_PROJ, D_MODEL), row_blk),
            pl.BlockSpec((1, D_MODEL), const2),
            pl.BlockSpec((D_MODEL, OFF_MG), const2),
            pl.BlockSpec((1, LANES), const2),
            pl.BlockSpec((1, LANES), const2),
        ],
        out_specs=(
            pl.BlockSpec((1, 2 * N_HEADS, TM_PROJ, LANES), lambda b, i: (b, 0, i, 0)),
            pl.BlockSpec((1, 2 * N_HEADS, TM_PROJ, LANES), lambda b, i: (b, 0, i, 0)),
            pl.BlockSpec((1, N_HEADS, V_DIM, TM_PROJ), lambda b, i: (b, 0, 0, i)),
            pl.BlockSpec((1, TM_PROJ, ATTN_WIDTH), row_blk),
            pl.BlockSpec((1, TM_PROJ, CONV_WIDTH), row_blk),
            pl.BlockSpec((1, TM_PROJ, CONV_WIDTH), row_blk),
        ),
        out_shape=out_shape,
        compiler_params=pltpu.CompilerParams(
            dimension_semantics=("arbitrary", "arbitrary"),
            vmem_limit_bytes=VMEM_LIMIT_BYTES),
        name="proj",
    )(x, g_pre, w_bf, g_q2, g_k2)


def _attn_kernel(cs_ref, q1_ref, q2_ref, k1_ref, k2_ref, vt_ref, sag_ref, gs_ref, lamp_ref,
                 o_ref, acc1_ref, acc2_ref, *, lam_init):
    n_kb = k1_ref.shape[0] // TK
    hh = pl.program_id(1)
    qi = pl.program_id(2)
    cs = cs_ref[hh]
    i0 = qi * TQ
    kd = i0 // TK
    off = i0 - kd * TK

    lane = lax.broadcasted_iota(jnp.int32, (TQ, LANES), 1)
    q1l = q1_ref[...]
    q2l = q2_ref[...]
    q1r = jnp.where(lane < HEAD_DIM, q1l, -q1l)
    q2r = jnp.where(lane < HEAD_DIM, -q2l, q2l)

    acc1_ref[...] = jnp.zeros_like(acc1_ref)
    acc2_ref[...] = jnp.zeros_like(acc2_ref)

    def one(s, m, l, acc_ref, vt, c):
        m_new = jnp.maximum(m, jnp.max(s, axis=0, keepdims=True) - c)
        p = jnp.exp2(s - (m_new + c))
        alpha = jnp.exp2(m - m_new)
        l_new = alpha * l + jnp.sum(p, axis=0, keepdims=True)
        acc_ref[...] = alpha * acc_ref[...] + jnp.dot(
            vt, p.astype(jnp.bfloat16), preferred_element_type=jnp.float32)
        return m_new, l_new

    def step(kb, carry, q1, q2, c, corr):
        m1, l1, m2, l2 = carry
        j0 = pl.multiple_of(kb * TK, TK)
        vt = vt_ref[:, pl.ds(j0, TK)]
        dn = (((1,), (1,)), ((), ()))
        s1 = lax.dot_general(k1_ref[pl.ds(j0, TK), :], q1, dn, preferred_element_type=jnp.float32)
        s2 = lax.dot_general(k2_ref[pl.ds(j0, TK), :], q2, dn, preferred_element_type=jnp.float32)
        if corr is not None:
            s1 = s1 - corr
            s2 = s2 - corr
        m1, l1 = one(s1, m1, l1, acc1_ref, vt, c)
        m2, l2 = one(s2, m2, l2, acc2_ref, vt, c)
        return m1, l1, m2, l2

    init = (jnp.full((1, TQ), NEG_INIT, jnp.float32), jnp.zeros((1, TQ), jnp.float32),
            jnp.full((1, TQ), NEG_INIT, jnp.float32), jnp.zeros((1, TQ), jnp.float32))

    carry = lax.fori_loop(
        0, kd,
        lambda kb, cr: step(kb, cr, q1l, q2l, cs * (i0 - kb * TK).astype(jnp.float32), None),
        init)

    dji = (lax.broadcasted_iota(jnp.int32, (TK, TQ), 0)
           - lax.broadcasted_iota(jnp.int32, (TK, TQ), 1) - off)
    corr = (2.0 * cs) * jnp.maximum(dji, 0).astype(jnp.float32)
    carry = step(kd, carry, q1l, q2l, cs * off.astype(jnp.float32), corr)

    carry = lax.fori_loop(
        kd + 1, n_kb,
        lambda kb, cr: step(kb, cr, q1r, q2r, cs * (kb * TK - i0).astype(jnp.float32), None),
        carry)
    m1, l1, m2, l2 = carry

    lp = lamp_ref[...]
    lam = (jnp.exp(jnp.sum(lp[0:1] * lp[1:2], axis=-1, keepdims=True))
           - jnp.exp(jnp.sum(lp[2:3] * lp[3:4], axis=-1, keepdims=True)) + lam_init)
    o = acc1_ref[...] / l1 - lam * (acc2_ref[...] / l2)
    o = o * lax.rsqrt(jnp.mean(o * o, axis=0, keepdims=True) + EPS) * gs_ref[...]
    o = o * (1.0 - lam_init)
    o_ref[0] = (o.T * sag_ref[0]).astype(jnp.bfloat16)


def _attn_call(cs_tab, qa, ka, vt, sag, gs, lamp, lam_init):
    B, _, S, _ = qa.shape
    grid = (B, N_HEADS, S // TQ)
    return pl.pallas_call(
        functools.partial(_attn_kernel, lam_init=lam_init),
        grid=grid,
        in_specs=[
            pl.BlockSpec(memory_space=pltpu.SMEM),
            pl.BlockSpec((None, None, TQ, LANES), lambda b, h, i: (b, 2 * h, i, 0)),
            pl.BlockSpec((None, None, TQ, LANES), lambda b, h, i: (b, 2 * h + 1, i, 0)),
            pl.BlockSpec((None, None, S, LANES), lambda b, h, i: (b, 2 * h, 0, 0)),
            pl.BlockSpec((None, None, S, LANES), lambda b, h, i: (b, 2 * h + 1, 0, 0)),
            pl.BlockSpec((None, None, V_DIM, S), lambda b, h, i: (b, h, 0, 0)),
            pl.BlockSpec((1, TQ, V_DIM), lambda b, h, i: (b, i, h)),
            pl.BlockSpec((V_DIM, 1), lambda b, h, i: (0, 0)),
            pl.BlockSpec((4, HEAD_DIM), lambda b, h, i: (0, 0)),
        ],
        out_specs=pl.BlockSpec((1, TQ, V_DIM), lambda b, h, i: (b, i, h)),
        out_shape=jax.ShapeDtypeStruct((B, S, ATTN_WIDTH), jnp.bfloat16),
        scratch_shapes=[pltpu.VMEM((V_DIM, TQ), jnp.float32),
                        pltpu.VMEM((V_DIM, TQ), jnp.float32)],
        compiler_params=pltpu.CompilerParams(
            dimension_semantics=("arbitrary", "arbitrary", "arbitrary"),
            vmem_limit_bytes=VMEM_LIMIT_BYTES),
        name="diff_attn",
    )(cs_tab, qa, qa, ka, ka, vt, sag, gs, lamp)


def _merge_kernel(x_ref, og_ref, up_ref, uc_ref, un_ref, scg_ref, gpre_ref, wmg_ref,
                  wao_ref, wco_ref, wout_ref, wdw_ref, bdw_ref, gcn_ref, bcn_ref,
                  o_ref, ucat_ref):
    tm = x_ref.shape[1]
    i = pl.program_id(1)
    n_i = pl.num_programs(1)

    ucat_ref[0:HALO] = jnp.where(i > 0, up_ref[0], 0.0)
    ucat_ref[HALO:HALO + tm] = uc_ref[0]
    ucat_ref[HALO + tm:HALO + tm + HALO] = jnp.where(i < n_i - 1, un_ref[0], 0.0)
    y = jnp.zeros((tm, CONV_WIDTH), jnp.float32) + bdw_ref[...]
    for t in range(CONV_KERNEL):
        lo = HALO - CONV_PAD + t
        y = y + ucat_ref[lo:lo + tm] * wdw_ref[t:t + 1]
    mu = jnp.mean(y, axis=-1, keepdims=True)
    yc = y - mu
    yn = yc * lax.rsqrt(jnp.mean(yc * yc, axis=-1, keepdims=True) + EPS) * gcn_ref[...] + bcn_ref[...]
    ub = (_silu(yn) * scg_ref[0]).astype(jnp.bfloat16)

    y_a = jnp.dot(og_ref[0], wao_ref[...], preferred_element_type=jnp.float32)
    y_b = jnp.dot(ub, wco_ref[...], preferred_element_type=jnp.float32)

    x = x_ref[0]
    h = x * lax.rsqrt(jnp.mean(x * x, axis=-1, keepdims=True) + EPS) * gpre_ref[...]
    zmg = jnp.dot(h.astype(jnp.bfloat16), wmg_ref[...], preferred_element_type=jnp.float32)
    gates = jax.nn.sigmoid(zmg)
    m = gates[:, :D_MODEL] * y_a + gates[:, D_MODEL:] * y_b
    o_ref[0] = x + jnp.dot(m.astype(jnp.bfloat16), wout_ref[...], preferred_element_type=jnp.float32)


def _merge_call(x, og, u, scg, g_pre, w_mg, w_ao, w_co, w_out, w_dw, b_dw, g_cn, b_cn):
    B, S, _ = x.shape
    n_i = S // TM_MERGE
    grid = (B, n_i)
    hpb = TM_MERGE // HALO
    n_hb = S // HALO
    row_blk = lambda b, i: (b, i, 0)
    const2 = lambda b, i: (0, 0)
    return pl.pallas_call(
        _merge_kernel,
        grid=grid,
        in_specs=[
            pl.BlockSpec((1, TM_MERGE, D_MODEL), row_blk),
            pl.BlockSpec((1, TM_MERGE, ATTN_WIDTH), row_blk),
            pl.BlockSpec((1, HALO, CONV_WIDTH), lambda b, i: (b, jnp.maximum(i * hpb - 1, 0), 0)),
            pl.BlockSpec((1, TM_MERGE, CONV_WIDTH), row_blk),
            pl.BlockSpec((1, HALO, CONV_WIDTH), lambda b, i: (b, jnp.minimum((i + 1) * hpb, n_hb - 1), 0)),
            pl.BlockSpec((1, TM_MERGE, CONV_WIDTH), row_blk),
            pl.BlockSpec((1, D_MODEL), const2),
            pl.BlockSpec((D_MODEL, 2 * D_MODEL), const2),
            pl.BlockSpec((ATTN_WIDTH, D_MODEL), const2),
            pl.BlockSpec((CONV_WIDTH, D_MODEL), const2),
            pl.BlockSpec((D_MODEL, D_MODEL), const2),
            pl.BlockSpec((CONV_KERNEL, CONV_WIDTH), const2),
            pl.BlockSpec((1, CONV_WIDTH), const2),
            pl.BlockSpec((1, CONV_WIDTH), const2),
            pl.BlockSpec((1, CONV_WIDTH), const2),
        ],
        out_specs=pl.BlockSpec((1, TM_MERGE, D_MODEL), row_blk),
        out_shape=jax.ShapeDtypeStruct((B, S, D_MODEL), jnp.float32),
        scratch_shapes=[pltpu.VMEM((TM_MERGE + 2 * HALO, CONV_WIDTH), jnp.float32)],
        compiler_params=pltpu.CompilerParams(
            dimension_semantics=("arbitrary", "arbitrary"),
            vmem_limit_bytes=VMEM_LIMIT_BYTES),
        name="merge_out",
    )(x, og, u, u, u, scg, g_pre, w_mg, w_ao, w_co, w_out, w_dw, b_dw, g_cn, b_cn)


def _lambda_init(layer_idx):
    return 0.8 - 0.6 * math.exp(-0.3 * layer_idx)


def kernel(x, g_pre, w_in, g_q, g_k, lam_q1, lam_k1, lam_q2, lam_k2, g_subln, w_attn_out, w_dw, b_dw, g_cn, b_cn, w_conv_out, w_out):
    B, S, D = x.shape
    depth = w_in.shape[0]
    assert D == D_MODEL and w_in.shape[2] == D_IN
    assert S % TK == 0 and S % TM_PROJ == 0 and S % TM_MERGE == 0

    bf = jnp.bfloat16
    cs_tab = jnp.asarray([s * LOG2E for s in ALIBI_SLOPES], jnp.float32)
    for l in range(depth):
        w_l = w_in[l].astype(bf)
        qa, ka, vt, sag, u, scg = _proj_call(
            x, g_pre[l][None, :], w_l[:, :OFF_MG],
            jnp.tile(g_q[l], 2)[None, :], jnp.tile(g_k[l], 2)[None, :])
        lamp = jnp.stack([lam_q1[l], lam_k1[l], lam_q2[l], lam_k2[l]]).astype(jnp.float32)
        og = _attn_call(cs_tab, qa, ka, vt, sag, g_subln[l][:, None], lamp, _lambda_init(l))
        x = _merge_call(
            x, og, u, scg, g_pre[l][None, :], w_l[:, OFF_MG:],
            w_attn_out[l].astype(bf), w_conv_out[l].astype(bf), w_out[l].astype(bf),
            w_dw[l], b_dw[l][None, :], g_cn[l][None, :], b_cn[l][None, :])
    return x
```

```python
import functools
import math

import numpy as np
import jax
import jax.numpy as jnp
from jax import lax
from jax.experimental import pallas as pl
from jax.experimental.pallas import tpu as pltpu

D_MODEL = 1024
HEAD_DIM = 64
V_DIM = 2 * HEAD_DIM
ATTN_WIDTH = D_MODEL // 2
N_HEADS = ATTN_WIDTH // V_DIM
CONV_WIDTH = D_MODEL // 2
CONV_KERNEL = 31
CONV_PAD = (CONV_KERNEL - 1) // 2
EPS = 1e-6
ALIBI_SLOPES = tuple(2.0 ** (-8.0 * (i + 1) / N_HEADS) for i in range(N_HEADS))
LOG2E = math.log2(math.e)

OFF_Q = 0
OFF_K = OFF_Q + N_HEADS * 2 * HEAD_DIM
OFF_V = OFF_K + N_HEADS * 2 * HEAD_DIM
OFF_AG = OFF_V + ATTN_WIDTH
OFF_GLU = OFF_AG + ATTN_WIDTH
OFF_CG = OFF_GLU + 2 * CONV_WIDTH
OFF_MG = OFF_CG + CONV_WIDTH
D_IN = OFF_MG + 2 * D_MODEL

LANES = 128
SUBLANES = 8
VMEM_LIMIT_BYTES = 56 * 1024 * 1024

TM_PROJ = 256
TM_MERGE = 256
TQ = 512
TK = 1024
SM_ROWS = 32
HALO = 16
JL_BITS = 5

N_AUG = 9
NEG_INIT = -1e30

assert TK % TQ == 0 and (TK >> JL_BITS) <= 256 and HALO >= CONV_PAD


def _bf16_pieces(v, n=3):
    out = []
    r = v
    for _ in range(n):
        p = r.astype(jnp.bfloat16).astype(jnp.float32)
        out.append(p)
        r = r - p
    return out


def _bf16_pieces_py(v, n=3):
    out = []
    r = float(v)
    for _ in range(n):
        p = float(np.asarray(r, np.float32).astype(jnp.bfloat16).astype(np.float32))
        out.append(p)
        r = r - p
    return out


def _silu(x):
    return x * jax.nn.sigmoid(x)


def _proj_kernel(x_ref, gpre_ref, w_ref, gq_ref, gk_ref,
                 qa_ref, ka_ref, vt_ref, sag_ref, u_ref, scg_ref):
    tm = x_ref.shape[1]
    x = x_ref[0]
    h = x * lax.rsqrt(jnp.mean(x * x, axis=-1, keepdims=True) + EPS) * gpre_ref[...]
    h = h.astype(jnp.bfloat16)

    def proj(lo, width):
        return jnp.dot(h, w_ref[:, lo:lo + width], preferred_element_type=jnp.float32)

    lane = lax.broadcasted_iota(jnp.int32, (tm, LANES), 1)
    row = pl.program_id(1) * tm + lax.broadcasted_iota(jnp.int32, (tm, LANES), 0)
    lo_half = lane < HEAD_DIM
    aug_idx = lane & (HEAD_DIM - 1)

    j_rel = row & (TK - 1)
    jh = (j_rel >> JL_BITS).astype(jnp.float32)
    jl = (j_rel & ((1 << JL_BITS) - 1)).astype(jnp.float32)
    aug_k = jnp.where(aug_idx < 3, jh,
                      jnp.where(aug_idx < 6, jl,
                                jnp.where(aug_idx < N_AUG, 1.0, 0.0)))
    i_rel = (row & (TQ - 1)).astype(jnp.float32)

    def qk_norm(z, g):
        zz = z * z
        s_lo = jnp.sum(jnp.where(lo_half, zz, 0.0), axis=-1, keepdims=True)
        s_hi = jnp.sum(jnp.where(lo_half, 0.0, zz), axis=-1, keepdims=True)
        r = jnp.where(lo_half, lax.rsqrt(s_lo * (1.0 / HEAD_DIM) + EPS),
                      lax.rsqrt(s_hi * (1.0 / HEAD_DIM) + EPS))
        return z * r * g

    zq = proj(OFF_Q, N_HEADS * 2 * HEAD_DIM)
    zk = proj(OFF_K, N_HEADS * 2 * HEAD_DIM)
    q_scale = LOG2E / math.sqrt(HEAD_DIM)
    for hh in range(N_HEADS):
        cs = ALIBI_SLOPES[hh] * LOG2E
        c = _bf16_pieces_py(cs)
        t = _bf16_pieces(-cs * i_rel)
        hi_w = float(1 << JL_BITS)
        aug_q = jnp.where(aug_idx == 0, hi_w * c[0], jnp.where(aug_idx == 1, hi_w * c[1],
                jnp.where(aug_idx == 2, hi_w * c[2], jnp.where(aug_idx == 3, c[0],
                jnp.where(aug_idx == 4, c[1], jnp.where(aug_idx == 5, c[2],
                jnp.where(aug_idx == 6, t[0], jnp.where(aug_idx == 7, t[1],
                jnp.where(aug_idx == 8, t[2], 0.0)))))))))
        sl = slice(hh * LANES, (hh + 1) * LANES)
        qn = qk_norm(zq[:, sl], gq_ref[...]) * q_scale
        kn = qk_norm(zk[:, sl], gk_ref[...])
        qa_ref[0, 2 * hh] = jnp.where(lo_half, qn, aug_q).astype(jnp.bfloat16)
        qa_ref[0, 2 * hh + 1] = jnp.where(lo_half, aug_q, qn).astype(jnp.bfloat16)
        ka_ref[0, 2 * hh] = jnp.where(lo_half, kn, aug_k).astype(jnp.bfloat16)
        ka_ref[0, 2 * hh + 1] = jnp.where(lo_half, aug_k, kn).astype(jnp.bfloat16)

    zv = proj(OFF_V, ATTN_WIDTH)
    for hh in range(N_HEADS):
        vt_ref[0, hh] = zv[:, hh * V_DIM:(hh + 1) * V_DIM].T.astype(jnp.bfloat16)

    sag_ref[0] = _silu(proj(OFF_AG, ATTN_WIDTH))
    zg = proj(OFF_GLU, 2 * CONV_WIDTH)
    u_ref[0] = zg[:, :CONV_WIDTH] * jax.nn.sigmoid(zg[:, CONV_WIDTH:])
    scg_ref[0] = _silu(proj(OFF_CG, CONV_WIDTH))


def _proj_call(x, g_pre, w_bf, g_q2, g_k2):
    B, S, _ = x.shape
    grid = (B, S // TM_PROJ)
    row_blk = lambda b, i: (b, i, 0)
    const2 = lambda b, i: (0, 0)
    out_shape = (
        jax.ShapeDtypeStruct((B, 2 * N_HEADS, S, LANES), jnp.bfloat16),
        jax.ShapeDtypeStruct((B, 2 * N_HEADS, S, LANES), jnp.bfloat16),
        jax.ShapeDtypeStruct((B, N_HEADS, V_DIM, S), jnp.bfloat16),
        jax.ShapeDtypeStruct((B, S, ATTN_WIDTH), jnp.float32),
        jax.ShapeDtypeStruct((B, S, CONV_WIDTH), jnp.float32),
        jax.ShapeDtypeStruct((B, S, CONV_WIDTH), jnp.float32),
    )
    return pl.pallas_call(
        _proj_kernel,
        grid=grid,
        in_specs=[
            pl.BlockSpec((1, TM_PROJ, D_MODEL), row_blk),
            pl.BlockSpec((1, D_MODEL), const2),
            pl.BlockSpec((D_MODEL, OFF_MG), const2),
            pl.BlockSpec((1, LANES), const2),
            pl.BlockSpec((1, LANES), const2),
        ],
        out_specs=(
            pl.BlockSpec((1, 2 * N_HEADS, TM_PROJ, LANES), lambda b, i: (b, 0, i, 0)),
            pl.BlockSpec((1, 2 * N_HEADS, TM_PROJ, LANES), lambda b, i: (b, 0, i, 0)),
            pl.BlockSpec((1, N_HEADS, V_DIM, TM_PROJ), lambda b, i: (b, 0, 0, i)),
            pl.BlockSpec((1, TM_PROJ, ATTN_WIDTH), row_blk),
            pl.BlockSpec((1, TM_PROJ, CONV_WIDTH), row_blk),
            pl.BlockSpec((1, TM_PROJ, CONV_WIDTH), row_blk),
        ),
        out_shape=out_shape,
        compiler_params=pltpu.CompilerParams(
            dimension_semantics=("arbitrary", "arbitrary"),
            vmem_limit_bytes=VMEM_LIMIT_BYTES),
        name="proj",
    )(x, g_pre, w_bf, g_q2, g_k2)


def _attn_kernel(cs_ref, q1_ref, q2_ref, k1_ref, k2_ref, vt_ref, sag_ref, gs_ref, lamp_ref,
                 o_ref, qs_ref, sa_ref, sb_ref, pa_ref, pb_ref, mla_ref, mlb_ref, acc_ref,
                 *, lam_init):
    k_refs = (k1_ref, k2_ref)
    s_refs = (sa_ref, sb_ref)
    p_refs = (pa_ref, pb_ref)
    ml_refs = (mla_ref, mlb_ref)
    n_kb = k1_ref.shape[0] // TK
    assert n_kb >= 2 and n_kb % 2 == 0
    hh = pl.program_id(1)
    qi = pl.program_id(2)
    cs = cs_ref[hh]
    i0 = qi * TQ
    kd = i0 // TK
    off = i0 - kd * TK

    lane = lax.broadcasted_iota(jnp.int32, (TQ, LANES), 1)
    q1l = q1_ref[...]
    q2l = q2_ref[...]
    qs_ref[0, 0] = q1l
    qs_ref[0, 1] = q2l
    qs_ref[1, 0] = jnp.where(lane < HEAD_DIM, q1l, -q1l)
    qs_ref[1, 1] = jnp.where(lane < HEAD_DIM, -q2l, q2l)
    acc_ref[...] = jnp.zeros_like(acc_ref)

    def key_start(t):
        return t * TK if isinstance(t, int) else pl.multiple_of(t * TK, TK)

    def qk(t, slot):
        side = jnp.where(t > kd, 1, 0)
        j0 = key_start(t)
        dn = (((1,), (1,)), ((), ()))
        for c in range(2):
            s = lax.dot_general(k_refs[c][pl.ds(j0, TK), :], qs_ref[side, c], dn,
                                preferred_element_type=jnp.float32)
            s_refs[slot][c] = s
            ml_refs[slot][c] = jnp.max(s, axis=0, keepdims=True)

    def fix_diag(t, slot):
        @pl.when(t == kd)
        def _():
            dji = (lax.broadcasted_iota(jnp.int32, (TK, TQ), 0)
                   - lax.broadcasted_iota(jnp.int32, (TK, TQ), 1) - off)
            corr = (2.0 * cs) * jnp.maximum(dji, 0).astype(jnp.float32)
            for c in range(2):
                s = s_refs[slot][c] - corr
                s_refs[slot][c] = s
                ml_refs[slot][c] = jnp.max(s, axis=0, keepdims=True)

    def sm(t, slot, m, l):
        cblk = cs * jnp.abs(i0 - t * TK).astype(jnp.float32)
        m_out, l_out, a_out = [], [], []
        for c in range(2):
            m_new = jnp.maximum(m[c], ml_refs[slot][c] - cblk)
            shift = m_new + cblk
            alpha = jnp.exp2(m[c] - m_new)
            lsum = jnp.zeros((SUBLANES, TQ), jnp.float32)
            for r in range(0, TK, SM_ROWS):
                pc = jnp.exp2(s_refs[slot][c, r:r + SM_ROWS, :] - shift)
                for r8 in range(0, SM_ROWS, SUBLANES):
                    lsum = lsum + pc[r8:r8 + SUBLANES]
                p_refs[slot][c, r:r + SM_ROWS, :] = pc.astype(jnp.bfloat16)
            l_out.append(alpha * l[c] + jnp.sum(lsum, axis=0, keepdims=True))
            m_out.append(m_new)
            a_out.append(alpha)
        return tuple(m_out), tuple(l_out), tuple(a_out)

    def av(t, slot, alpha):
        vt = vt_ref[:, pl.ds(key_start(t), TK)]
        for c in range(2):
            acc_ref[c] = alpha[c] * acc_ref[c] + jnp.dot(
                vt, p_refs[slot][c], preferred_element_type=jnp.float32)

    neg = jnp.full((1, TQ), NEG_INIT, jnp.float32)
    zero = jnp.zeros((1, TQ), jnp.float32)
    m, l = (neg, neg), (zero, zero)

    qk(0, 0)
    fix_diag(0, 0)
    qk(1, 1)
    m, l, alpha = sm(0, 0, m, l)

    def half(t, slot, m, l, alpha):
        fix_diag(t - 1, 1 - slot)
        qk(t, slot)
        m, l, alpha_new = sm(t - 1, 1 - slot, m, l)
        av(t - 2, slot, alpha)
        return m, l, alpha_new

    def pair(g, carry):
        t = 2 + 2 * g
        carry = half(t, 0, *carry)
        return half(t + 1, 1, *carry)

    m, l, alpha = lax.fori_loop(0, (n_kb - 2) // 2, pair, (m, l, alpha))

    fix_diag(n_kb - 1, 1)
    m, l, alpha_last = sm(n_kb - 1, 1, m, l)
    av(n_kb - 2, 0, alpha)
    av(n_kb - 1, 1, alpha_last)

    lp = lamp_ref[...]
    lam = (jnp.exp(jnp.sum(lp[0:1] * lp[1:2], axis=-1, keepdims=True))
           - jnp.exp(jnp.sum(lp[2:3] * lp[3:4], axis=-1, keepdims=True)) + lam_init)
    o = acc_ref[0] / l[0] - lam * (acc_ref[1] / l[1])
    o = o * lax.rsqrt(jnp.mean(o * o, axis=0, keepdims=True) + EPS) * gs_ref[...]
    o = o * (1.0 - lam_init)
    o_ref[0] = (o.T * sag_ref[0]).astype(jnp.bfloat16)


def _attn_call(cs_tab, qa, ka, vt, sag, gs, lamp, lam_init):
    B, _, S, _ = qa.shape
    grid = (B, N_HEADS, S // TQ)
    return pl.pallas_call(
        functools.partial(_attn_kernel, lam_init=lam_init),
        grid=grid,
        in_specs=[
            pl.BlockSpec(memory_space=pltpu.SMEM),
            pl.BlockSpec((None, None, TQ, LANES), lambda b, h, i: (b, 2 * h, i, 0)),
            pl.BlockSpec((None, None, TQ, LANES), lambda b, h, i: (b, 2 * h + 1, i, 0)),
            pl.BlockSpec((None, None, S, LANES), lambda b, h, i: (b, 2 * h, 0, 0)),
            pl.BlockSpec((None, None, S, LANES), lambda b, h, i: (b, 2 * h + 1, 0, 0)),
            pl.BlockSpec((None, None, V_DIM, S), lambda b, h, i: (b, h, 0, 0)),
            pl.BlockSpec((1, TQ, V_DIM), lambda b, h, i: (b, i, h)),
            pl.BlockSpec((V_DIM, 1), lambda b, h, i: (0, 0)),
            pl.BlockSpec((4, HEAD_DIM), lambda b, h, i: (0, 0)),
        ],
        out_specs=pl.BlockSpec((1, TQ, V_DIM), lambda b, h, i: (b, i, h)),
        out_shape=jax.ShapeDtypeStruct((B, S, ATTN_WIDTH), jnp.bfloat16),
        scratch_shapes=[
            pltpu.VMEM((2, 2, TQ, LANES), jnp.bfloat16),
            pltpu.VMEM((2, TK, TQ), jnp.float32),
            pltpu.VMEM((2, TK, TQ), jnp.float32),
            pltpu.VMEM((2, TK, TQ), jnp.bfloat16),
            pltpu.VMEM((2, TK, TQ), jnp.bfloat16),
            pltpu.VMEM((2, 1, TQ), jnp.float32),
            pltpu.VMEM((2, 1, TQ), jnp.float32),
            pltpu.VMEM((2, V_DIM, TQ), jnp.float32),
        ],
        compiler_params=pltpu.CompilerParams(
            dimension_semantics=("arbitrary", "arbitrary", "arbitrary"),
            vmem_limit_bytes=VMEM_LIMIT_BYTES),
        name="diff_attn",
    )(cs_tab, qa, qa, ka, ka, vt, sag, gs, lamp)


def _merge_kernel(x_ref, og_ref, up_ref, uc_ref, un_ref, scg_ref, gpre_ref, wmg_ref,
                  wao_ref, wco_ref, wout_ref, wdw_ref, bdw_ref, gcn_ref, bcn_ref,
                  o_ref, ucat_ref):
    tm = x_ref.shape[1]
    i = pl.program_id(1)
    n_i = pl.num_programs(1)

    ucat_ref[0:HALO] = jnp.where(i > 0, up_ref[0], 0.0)
    ucat_ref[HALO:HALO + tm] = uc_ref[0]
    ucat_ref[HALO + tm:HALO + tm + HALO] = jnp.where(i < n_i - 1, un_ref[0], 0.0)
    y = jnp.zeros((tm, CONV_WIDTH), jnp.float32) + bdw_ref[...]
    for t in range(CONV_KERNEL):
        lo = HALO - CONV_PAD + t
        y = y + ucat_ref[lo:lo + tm] * wdw_ref[t:t + 1]
    mu = jnp.mean(y, axis=-1, keepdims=True)
    yc = y - mu
    yn = yc * lax.rsqrt(jnp.mean(yc * yc, axis=-1, keepdims=True) + EPS) * gcn_ref[...] + bcn_ref[...]
    ub = (_silu(yn) * scg_ref[0]).astype(jnp.bfloat16)

    y_a = jnp.dot(og_ref[0], wao_ref[...], preferred_element_type=jnp.float32)
    y_b = jnp.dot(ub, wco_ref[...], preferred_element_type=jnp.float32)

    x = x_ref[0]
    h = x * lax.rsqrt(jnp.mean(x * x, axis=-1, keepdims=True) + EPS) * gpre_ref[...]
    zmg = jnp.dot(h.astype(jnp.bfloat16), wmg_ref[...], preferred_element_type=jnp.float32)
    gates = jax.nn.sigmoid(zmg)
    m = gates[:, :D_MODEL] * y_a + gates[:, D_MODEL:] * y_b
    o_ref[0] = x + jnp.dot(m.astype(jnp.bfloat16), wout_ref[...], preferred_element_type=jnp.float32)


def _merge_call(x, og, u, scg, g_pre, w_mg, w_ao, w_co, w_out, w_dw, b_dw, g_cn, b_cn):
    B, S, _ = x.shape
    n_i = S // TM_MERGE
    grid = (B, n_i)
    hpb = TM_MERGE // HALO
    n_hb = S // HALO
    row_blk = lambda b, i: (b, i, 0)
    const2 = lambda b, i: (0, 0)
    return pl.pallas_call(
        _merge_kernel,
        grid=grid,
        in_specs=[
            pl.BlockSpec((1, TM_MERGE, D_MODEL), row_blk),
            pl.BlockSpec((1, TM_MERGE, ATTN_WIDTH), row_blk),
            pl.BlockSpec((1, HALO, CONV_WIDTH), lambda b, i: (b, jnp.maximum(i * hpb - 1, 0), 0)),
            pl.BlockSpec((1, TM_MERGE, CONV_WIDTH), row_blk),
            pl.BlockSpec((1, HALO, CONV_WIDTH), lambda b, i: (b, jnp.minimum((i + 1) * hpb, n_hb - 1), 0)),
            pl.BlockSpec((1, TM_MERGE, CONV_WIDTH), row_blk),
            pl.BlockSpec((1, D_MODEL), const2),
            pl.BlockSpec((D_MODEL, 2 * D_MODEL), const2),
            pl.BlockSpec((ATTN_WIDTH, D_MODEL), const2),
            pl.BlockSpec((CONV_WIDTH, D_MODEL), const2),
            pl.BlockSpec((D_MODEL, D_MODEL), const2),
            pl.BlockSpec((CONV_KERNEL, CONV_WIDTH), const2),
            pl.BlockSpec((1, CONV_WIDTH), const2),
            pl.BlockSpec((1, CONV_WIDTH), const2),
            pl.BlockSpec((1, CONV_WIDTH), const2),
        ],
        out_specs=pl.BlockSpec((1, TM_MERGE, D_MODEL), row_blk),
        out_shape=jax.ShapeDtypeStruct((B, S, D_MODEL), jnp.float32),
        scratch_shapes=[pltpu.VMEM((TM_MERGE + 2 * HALO, CONV_WIDTH), jnp.float32)],
        compiler_params=pltpu.CompilerParams(
            dimension_semantics=("arbitrary", "arbitrary"),
            vmem_limit_bytes=VMEM_LIMIT_BYTES),
        name="merge_out",
    )(x, og, u, u, u, scg, g_pre, w_mg, w_ao, w_co, w_out, w_dw, b_dw, g_cn, b_cn)


def _lambda_init(layer_idx):
    return 0.8 - 0.6 * math.exp(-0.3 * layer_idx)


def kernel(x, g_pre, w_in, g_q, g_k, lam_q1, lam_k1, lam_q2, lam_k2, g_subln, w_attn_out, w_dw, b_dw, g_cn, b_cn, w_conv_out, w_out):
    B, S, D = x.shape
    depth = w_in.shape[0]
    assert D == D_MODEL and w_in.shape[2] == D_IN
    assert S % TK == 0 and S % TM_PROJ == 0 and S % TM_MERGE == 0

    bf = jnp.bfloat16
    cs_tab = jnp.asarray([s * LOG2E for s in ALIBI_SLOPES], jnp.float32)
    for l in range(depth):
        w_l = w_in[l].astype(bf)
        qa, ka, vt, sag, u, scg = _proj_call(
            x, g_pre[l][None, :], w_l[:, :OFF_MG],
            jnp.tile(g_q[l], 2)[None, :], jnp.tile(g_k[l], 2)[None, :])
        lamp = jnp.stack([lam_q1[l], lam_k1[l], lam_q2[l], lam_k2[l]]).astype(jnp.float32)
        og = _attn_call(cs_tab, qa, ka, vt, sag, g_subln[l][:, None], lamp, _lambda_init(l))
        x = _merge_call(
            x, og, u, scg, g_pre[l][None, :], w_l[:, OFF_MG:],
            w_attn_out[l].astype(bf), w_conv_out[l].astype(bf), w_out[l].astype(bf),
            w_dw[l], b_dw[l][None, :], g_cn[l][None, :], b_cn[l][None, :])
    return x
```

```python
import functools
import math

import numpy as np
import jax
import jax.numpy as jnp
from jax import lax
from jax.experimental import pallas as pl
from jax.experimental.pallas import tpu as pltpu

D_MODEL = 1024
HEAD_DIM = 64
V_DIM = 2 * HEAD_DIM
ATTN_WIDTH = D_MODEL // 2
N_HEADS = ATTN_WIDTH // V_DIM
CONV_WIDTH = D_MODEL // 2
CONV_KERNEL = 31
CONV_PAD = (CONV_KERNEL - 1) // 2
EPS = 1e-6
ALIBI_SLOPES = tuple(2.0 ** (-8.0 * (i + 1) / N_HEADS) for i in range(N_HEADS))
LOG2E = math.log2(math.e)

OFF_Q = 0
OFF_K = OFF_Q + N_HEADS * 2 * HEAD_DIM
OFF_V = OFF_K + N_HEADS * 2 * HEAD_DIM
OFF_AG = OFF_V + ATTN_WIDTH
OFF_GLU = OFF_AG + ATTN_WIDTH
OFF_CG = OFF_GLU + 2 * CONV_WIDTH
OFF_MG = OFF_CG + CONV_WIDTH
D_IN = OFF_MG + 2 * D_MODEL

LANES = 128
SUBLANES = 8
VMEM_LIMIT_BYTES = 56 * 1024 * 1024

TM_PROJ = 256
TM_MERGE = 256
TQ = 512
TK = 1024
SM_ROWS = 32
HALO = 16
JL_BITS = 5

N_AUG = 12
NORM_SAFETY = 1.01
FIXED_SHIFT_MAX_QK = 45.0
NEG_INIT = -1e30

assert TK % TQ == 0 and (TK >> JL_BITS) <= 256 and HALO >= CONV_PAD


def _bf16_pieces(v, n=3):
    out = []
    r = v
    for _ in range(n):
        p = r.astype(jnp.bfloat16).astype(jnp.float32)
        out.append(p)
        r = r - p
    return out


def _bf16_pieces_py(v, n=3):
    out = []
    r = float(v)
    for _ in range(n):
        p = float(np.asarray(r, np.float32).astype(jnp.bfloat16).astype(np.float32))
        out.append(p)
        r = r - p
    return out


def _silu(x):
    return x * jax.nn.sigmoid(x)


def _proj_kernel(x_ref, gpre_ref, w_ref, gq_ref, gk_ref,
                 qa_ref, ka_ref, vt_ref, sag_ref, u_ref, scg_ref, qsq_ref, ksq_ref):
    tm = x_ref.shape[1]
    x = x_ref[0]
    h = x * lax.rsqrt(jnp.mean(x * x, axis=-1, keepdims=True) + EPS) * gpre_ref[...]
    h = h.astype(jnp.bfloat16)

    def proj(lo, width):
        return jnp.dot(h, w_ref[:, lo:lo + width], preferred_element_type=jnp.float32)

    lane = lax.broadcasted_iota(jnp.int32, (tm, LANES), 1)
    row = pl.program_id(1) * tm + lax.broadcasted_iota(jnp.int32, (tm, LANES), 0)
    lo_half = lane < HEAD_DIM
    aug_idx = lane & (HEAD_DIM - 1)

    j_rel = row & (TK - 1)
    jh = (j_rel >> JL_BITS).astype(jnp.float32)
    jl = (j_rel & ((1 << JL_BITS) - 1)).astype(jnp.float32)
    jb = (row // TK).astype(jnp.float32)
    aug_k = jnp.where(aug_idx < 3, jh,
                      jnp.where(aug_idx < 6, jl,
                                jnp.where(aug_idx < 9, 1.0,
                                          jnp.where(aug_idx < N_AUG, jb, 0.0))))
    i_rel = (row & (TQ - 1)).astype(jnp.float32)

    def half_sums(v):
        return (jnp.sum(jnp.where(lo_half, v, 0.0), axis=-1, keepdims=True),
                jnp.sum(jnp.where(lo_half, 0.0, v), axis=-1, keepdims=True))

    def qk_norm(z, g):
        s_lo, s_hi = half_sums(z * z)
        r = jnp.where(lo_half, lax.rsqrt(s_lo * (1.0 / HEAD_DIM) + EPS),
                      lax.rsqrt(s_hi * (1.0 / HEAD_DIM) + EPS))
        return z * r * g

    def store_max_sqnorm(ref, hh, v):
        n_lo, n_hi = half_sums(v * v)
        ref[0, 0, 2 * hh:2 * hh + 1, :] = jnp.broadcast_to(jnp.max(n_lo, axis=0, keepdims=True), (1, LANES))
        ref[0, 0, 2 * hh + 1:2 * hh + 2, :] = jnp.broadcast_to(jnp.max(n_hi, axis=0, keepdims=True), (1, LANES))

    zq = proj(OFF_Q, N_HEADS * 2 * HEAD_DIM)
    zk = proj(OFF_K, N_HEADS * 2 * HEAD_DIM)
    q_scale = LOG2E / math.sqrt(HEAD_DIM)
    for hh in range(N_HEADS):
        cs = ALIBI_SLOPES[hh] * LOG2E
        c = _bf16_pieces_py(cs)
        t = _bf16_pieces(-cs * i_rel)
        hi_w = float(1 << JL_BITS)
        aug_q = jnp.where(aug_idx == 0, hi_w * c[0], jnp.where(aug_idx == 1, hi_w * c[1],
                jnp.where(aug_idx == 2, hi_w * c[2], jnp.where(aug_idx == 3, c[0],
                jnp.where(aug_idx == 4, c[1], jnp.where(aug_idx == 5, c[2],
                jnp.where(aug_idx == 6, t[0], jnp.where(aug_idx == 7, t[1],
                jnp.where(aug_idx == 8, t[2], 0.0)))))))))
        sl = slice(hh * LANES, (hh + 1) * LANES)
        qn = qk_norm(zq[:, sl], gq_ref[...]) * q_scale
        kn = qk_norm(zk[:, sl], gk_ref[...])
        store_max_sqnorm(qsq_ref, hh, qn)
        store_max_sqnorm(ksq_ref, hh, kn)
        qa_ref[0, 2 * hh] = jnp.where(lo_half, qn, aug_q).astype(jnp.bfloat16)
        qa_ref[0, 2 * hh + 1] = jnp.where(lo_half, aug_q, qn).astype(jnp.bfloat16)
        ka_ref[0, 2 * hh] = jnp.where(lo_half, kn, aug_k).astype(jnp.bfloat16)
        ka_ref[0, 2 * hh + 1] = jnp.where(lo_half, aug_k, kn).astype(jnp.bfloat16)

    zv = proj(OFF_V, ATTN_WIDTH)
    for hh in range(N_HEADS):
        vt_ref[0, hh] = zv[:, hh * V_DIM:(hh + 1) * V_DIM].T.astype(jnp.bfloat16)

    sag_ref[0] = _silu(proj(OFF_AG, ATTN_WIDTH))
    zg = proj(OFF_GLU, 2 * CONV_WIDTH)
    u_ref[0] = zg[:, :CONV_WIDTH] * jax.nn.sigmoid(zg[:, CONV_WIDTH:])
    scg_ref[0] = _silu(proj(OFF_CG, CONV_WIDTH))


def _proj_call(x, g_pre, w_bf, g_q2, g_k2):
    B, S, _ = x.shape
    grid = (B, S // TM_PROJ)
    row_blk = lambda b, i: (b, i, 0)
    const2 = lambda b, i: (0, 0)
    out_shape = (
        jax.ShapeDtypeStruct((B, 2 * N_HEADS, S, LANES), jnp.bfloat16),
        jax.ShapeDtypeStruct((B, 2 * N_HEADS, S, LANES), jnp.bfloat16),
        jax.ShapeDtypeStruct((B, N_HEADS, V_DIM, S), jnp.bfloat16),
        jax.ShapeDtypeStruct((B, S, ATTN_WIDTH), jnp.float32),
        jax.ShapeDtypeStruct((B, S, CONV_WIDTH), jnp.float32),
        jax.ShapeDtypeStruct((B, S, CONV_WIDTH), jnp.float32),
        jax.ShapeDtypeStruct((B, S // TM_PROJ, 2 * N_HEADS, LANES), jnp.float32),
        jax.ShapeDtypeStruct((B, S // TM_PROJ, 2 * N_HEADS, LANES), jnp.float32),
    )
    return pl.pallas_call(
        _proj_kernel,
        grid=grid,
        in_specs=[
            pl.BlockSpec((1, TM_PROJ, D_MODEL), row_blk),
            pl.BlockSpec((1, D_MODEL), const2),
            pl.BlockSpec((D_MODEL, OFF_MG), const2),
            pl.BlockSpec((1, LANES), const2),
            pl.BlockSpec((1, LANES), const2),
        ],
        out_specs=(
            pl.BlockSpec((1, 2 * N_HEADS, TM_PROJ, LANES), lambda b, i: (b, 0, i, 0)),
            pl.BlockSpec((1, 2 * N_HEADS, TM_PROJ, LANES), lambda b, i: (b, 0, i, 0)),
            pl.BlockSpec((1, N_HEADS, V_DIM, TM_PROJ), lambda b, i: (b, 0, 0, i)),
            pl.BlockSpec((1, TM_PROJ, ATTN_WIDTH), row_blk),
            pl.BlockSpec((1, TM_PROJ, CONV_WIDTH), row_blk),
            pl.BlockSpec((1, TM_PROJ, CONV_WIDTH), row_blk),
            pl.BlockSpec((1, 1, 2 * N_HEADS, LANES), lambda b, i: (b, i, 0, 0)),
            pl.BlockSpec((1, 1, 2 * N_HEADS, LANES), lambda b, i: (b, i, 0, 0)),
        ),
        out_shape=out_shape,
        compiler_params=pltpu.CompilerParams(
            dimension_semantics=("arbitrary", "arbitrary"),
            vmem_limit_bytes=VMEM_LIMIT_BYTES),
        name="proj",
    )(x, g_pre, w_bf, g_q2, g_k2)


def _attn_kernel(cs_ref, q1_ref, q2_ref, k1_ref, k2_ref, vt_ref, sag_ref, gs_ref, lamp_ref,
                 o_ref, qs_ref, sa_ref, sb_ref, pa_ref, pb_ref, mla_ref, mlb_ref, acc_ref,
                 *, lam_init):
    k_refs = (k1_ref, k2_ref)
    s_refs = (sa_ref, sb_ref)
    p_refs = (pa_ref, pb_ref)
    ml_refs = (mla_ref, mlb_ref)
    n_kb = k1_ref.shape[0] // TK
    assert n_kb >= 2 and n_kb % 2 == 0
    hh = pl.program_id(1)
    qi = pl.program_id(2)
    cs = cs_ref[hh]
    i0 = qi * TQ
    kd = i0 // TK
    off = i0 - kd * TK

    lane = lax.broadcasted_iota(jnp.int32, (TQ, LANES), 1)
    q1l = q1_ref[...]
    q2l = q2_ref[...]
    qs_ref[0, 0] = q1l
    qs_ref[0, 1] = q2l
    qs_ref[1, 0] = jnp.where(lane < HEAD_DIM, q1l, -q1l)
    qs_ref[1, 1] = jnp.where(lane < HEAD_DIM, -q2l, q2l)
    acc_ref[...] = jnp.zeros_like(acc_ref)

    def key_start(t):
        return t * TK if isinstance(t, int) else pl.multiple_of(t * TK, TK)

    def qk(t, slot):
        side = jnp.where(t > kd, 1, 0)
        j0 = key_start(t)
        dn = (((1,), (1,)), ((), ()))
        for c in range(2):
            s = lax.dot_general(k_refs[c][pl.ds(j0, TK), :], qs_ref[side, c], dn,
                                preferred_element_type=jnp.float32)
            s_refs[slot][c] = s
            ml_refs[slot][c] = jnp.max(s, axis=0, keepdims=True)

    def fix_diag(t, slot):
        @pl.when(t == kd)
        def _():
            dji = (lax.broadcasted_iota(jnp.int32, (TK, TQ), 0)
                   - lax.broadcasted_iota(jnp.int32, (TK, TQ), 1) - off)
            corr = (2.0 * cs) * jnp.maximum(dji, 0).astype(jnp.float32)
            for c in range(2):
                s = s_refs[slot][c] - corr
                s_refs[slot][c] = s
                ml_refs[slot][c] = jnp.max(s, axis=0, keepdims=True)

    def sm(t, slot, m, l):
        cblk = cs * jnp.abs(i0 - t * TK).astype(jnp.float32)
        m_out, l_out, a_out = [], [], []
        for c in range(2):
            m_new = jnp.maximum(m[c], ml_refs[slot][c] - cblk)
            shift = m_new + cblk
            alpha = jnp.exp2(m[c] - m_new)
            lsum = jnp.zeros((SUBLANES, TQ), jnp.float32)
            for r in range(0, TK, SM_ROWS):
                pc = jnp.exp2(s_refs[slot][c, r:r + SM_ROWS, :] - shift)
                for r8 in range(0, SM_ROWS, SUBLANES):
                    lsum = lsum + pc[r8:r8 + SUBLANES]
                p_refs[slot][c, r:r + SM_ROWS, :] = pc.astype(jnp.bfloat16)
            l_out.append(alpha * l[c] + jnp.sum(lsum, axis=0, keepdims=True))
            m_out.append(m_new)
            a_out.append(alpha)
        return tuple(m_out), tuple(l_out), tuple(a_out)

    def av(t, slot, alpha):
        vt = vt_ref[:, pl.ds(key_start(t), TK)]
        for c in range(2):
            acc_ref[c] = alpha[c] * acc_ref[c] + jnp.dot(
                vt, p_refs[slot][c], preferred_element_type=jnp.float32)

    neg = jnp.full((1, TQ), NEG_INIT, jnp.float32)
    zero = jnp.zeros((1, TQ), jnp.float32)
    m, l = (neg, neg), (zero, zero)

    qk(0, 0)
    fix_diag(0, 0)
    qk(1, 1)
    m, l, alpha = sm(0, 0, m, l)

    def half(t, slot, m, l, alpha):
        fix_diag(t - 1, 1 - slot)
        qk(t, slot)
        m, l, alpha_new = sm(t - 1, 1 - slot, m, l)
        av(t - 2, slot, alpha)
        return m, l, alpha_new

    def pair(g, carry):
        t = 2 + 2 * g
        carry = half(t, 0, *carry)
        return half(t + 1, 1, *carry)

    m, l, alpha = lax.fori_loop(0, (n_kb - 2) // 2, pair, (m, l, alpha))

    fix_diag(n_kb - 1, 1)
    m, l, alpha_last = sm(n_kb - 1, 1, m, l)
    av(n_kb - 2, 0, alpha)
    av(n_kb - 1, 1, alpha_last)

    lp = lamp_ref[...]
    lam = (jnp.exp(jnp.sum(lp[0:1] * lp[1:2], axis=-1, keepdims=True))
           - jnp.exp(jnp.sum(lp[2:3] * lp[3:4], axis=-1, keepdims=True)) + lam_init)
    o = acc_ref[0] / l[0] - lam * (acc_ref[1] / l[1])
    o = o * lax.rsqrt(jnp.mean(o * o, axis=0, keepdims=True) + EPS) * gs_ref[...]
    o = o * (1.0 - lam_init)
    o_ref[0] = (o.T * sag_ref[0]).astype(jnp.bfloat16)


def _attn_call(cs_tab, qa, ka, vt, sag, gs, lamp, lam_init):
    B, _, S, _ = qa.shape
    grid = (B, N_HEADS, S // TQ)
    return pl.pallas_call(
        functools.partial(_attn_kernel, lam_init=lam_init),
        grid=grid,
        in_specs=[
            pl.BlockSpec(memory_space=pltpu.SMEM),
            pl.BlockSpec((None, None, TQ, LANES), lambda b, h, i: (b, 2 * h, i, 0)),
            pl.BlockSpec((None, None, TQ, LANES), lambda b, h, i: (b, 2 * h + 1, i, 0)),
            pl.BlockSpec((None, None, S, LANES), lambda b, h, i: (b, 2 * h, 0, 0)),
            pl.BlockSpec((None, None, S, LANES), lambda b, h, i: (b, 2 * h + 1, 0, 0)),
            pl.BlockSpec((None, None, V_DIM, S), lambda b, h, i: (b, h, 0, 0)),
            pl.BlockSpec((1, TQ, V_DIM), lambda b, h, i: (b, i, h)),
            pl.BlockSpec((V_DIM, 1), lambda b, h, i: (0, 0)),
            pl.BlockSpec((4, HEAD_DIM), lambda b, h, i: (0, 0)),
        ],
        out_specs=pl.BlockSpec((1, TQ, V_DIM), lambda b, h, i: (b, i, h)),
        out_shape=jax.ShapeDtypeStruct((B, S, ATTN_WIDTH), jnp.bfloat16),
        scratch_shapes=[
            pltpu.VMEM((2, 2, TQ, LANES), jnp.bfloat16),
            pltpu.VMEM((2, TK, TQ), jnp.float32),
            pltpu.VMEM((2, TK, TQ), jnp.float32),
            pltpu.VMEM((2, TK, TQ), jnp.bfloat16),
            pltpu.VMEM((2, TK, TQ), jnp.bfloat16),
            pltpu.VMEM((2, 1, TQ), jnp.float32),
            pltpu.VMEM((2, 1, TQ), jnp.float32),
            pltpu.VMEM((2, V_DIM, TQ), jnp.float32),
        ],
        compiler_params=pltpu.CompilerParams(
            dimension_semantics=("arbitrary", "arbitrary", "arbitrary"),
            vmem_limit_bytes=VMEM_LIMIT_BYTES),
        name="diff_attn",
    )(cs_tab, qa, qa, ka, ka, vt, sag, gs, lamp)


def _attn_fixed_kernel(cs_ref, csp_ref, kmax_ref, q1_ref, q2_ref, k1_ref, k2_ref, vt_ref, sag_ref,
                       gs_ref, lamp_ref, o_ref, qs_ref, pa_ref, pb_ref, acc_ref, *, lam_init):
    k_refs = (k1_ref, k2_ref)
    p_refs = (pa_ref, pb_ref)
    n_kb = k1_ref.shape[0] // TK
    assert n_kb >= 2 and n_kb % 2 == 0
    bb = pl.program_id(0)
    hh = pl.program_id(1)
    qi = pl.program_id(2)
    cs = cs_ref[hh]
    cp = [csp_ref[3 * hh + i] for i in range(3)]
    i0 = qi * TQ
    kd = i0 // TK
    off = i0 - kd * TK

    lane = lax.broadcasted_iota(jnp.int32, (TQ, LANES), 1)
    aug_idx = lane & (HEAD_DIM - 1)
    i_abs = (i0 + lax.broadcasted_iota(jnp.int32, (TQ, LANES), 0)).astype(jnp.float32)
    hi_w = float(1 << JL_BITS)
    for c, q_ref in enumerate((q1_ref, q2_ref)):
        q = q_ref[...].astype(jnp.float32)
        valid = (lane < HEAD_DIM) if c == 0 else (lane >= HEAD_DIM)
        qsq = jnp.sum(jnp.where(valid, q * q, 0.0), axis=-1, keepdims=True)
        bound = jnp.sqrt(qsq) * (kmax_ref[(bb * N_HEADS + hh) * 2 + c] * NORM_SAFETY)
        for side, sgn in ((0, 1.0), (1, -1.0)):
            t = _bf16_pieces(-sgn * cs * i_abs - bound)
            aug = jnp.zeros((TQ, LANES), jnp.float32)
            for i in range(3):
                aug = jnp.where(aug_idx == i, sgn * hi_w * cp[i], aug)
                aug = jnp.where(aug_idx == 3 + i, sgn * cp[i], aug)
                aug = jnp.where(aug_idx == 6 + i, t[i], aug)
                aug = jnp.where(aug_idx == 9 + i, sgn * float(TK) * cp[i], aug)
            qs_ref[side, c] = jnp.where(valid, q, aug).astype(jnp.bfloat16)
    acc_ref[...] = jnp.zeros_like(acc_ref)

    def key_start(t):
        return pl.multiple_of(t * TK, TK)

    def stage_a(t, slot, lsum, diag):
        side = jnp.where(t > kd, 1, 0)
        j0 = key_start(t)
        dn = (((1,), (1,)), ((), ()))
        out = []
        for c in range(2):
            s = lax.dot_general(k_refs[c][pl.ds(j0, TK), :], qs_ref[side, c], dn,
                                preferred_element_type=jnp.float32)
            if diag:
                dji = (lax.broadcasted_iota(jnp.int32, (TK, TQ), 0)
                       - lax.broadcasted_iota(jnp.int32, (TK, TQ), 1) - off)
                s = s - (2.0 * cs) * jnp.maximum(dji, 0).astype(jnp.float32)
            ls = lsum[c]
            for r in range(0, TK, SM_ROWS):
                pc = jnp.exp2(s[r:r + SM_ROWS, :])
                for r8 in range(0, SM_ROWS, SUBLANES):
                    ls = ls + pc[r8:r8 + SUBLANES]
                p_refs[slot][c, r:r + SM_ROWS, :] = pc.astype(jnp.bfloat16)
            out.append(ls)
        return tuple(out)

    def stage_b(t, slot):
        vt = vt_ref[:, pl.ds(key_start(t), TK)]
        for c in range(2):
            acc_ref[c] += jnp.dot(vt, p_refs[slot][c], preferred_element_type=jnp.float32)

    def nth(n):
        return (n - 1) + jnp.where(n - 1 >= kd, 1, 0)

    zero = jnp.zeros((SUBLANES, TQ), jnp.float32)
    lsum = stage_a(kd, 0, (zero, zero), True)
    lsum = stage_a(nth(1), 1, lsum, False)
    stage_b(kd, 0)

    def pair(g, lsum):
        n = 2 + 2 * g
        lsum = stage_a(nth(n), 0, lsum, False)
        stage_b(nth(n - 1), 1)
        lsum = stage_a(nth(n + 1), 1, lsum, False)
        stage_b(nth(n), 0)
        return lsum

    lsum = lax.fori_loop(0, (n_kb - 2) // 2, pair, lsum)
    stage_b(nth(n_kb - 1), 1)
    l = [jnp.sum(ls, axis=0, keepdims=True) for ls in lsum]

    lp = lamp_ref[...]
    lam = (jnp.exp(jnp.sum(lp[0:1] * lp[1:2], axis=-1, keepdims=True))
           - jnp.exp(jnp.sum(lp[2:3] * lp[3:4], axis=-1, keepdims=True)) + lam_init)
    o = acc_ref[0] / l[0] - lam * (acc_ref[1] / l[1])
    o = o * lax.rsqrt(jnp.mean(o * o, axis=0, keepdims=True) + EPS) * gs_ref[...]
    o = o * (1.0 - lam_init)
    o_ref[0] = (o.T * sag_ref[0]).astype(jnp.bfloat16)


def _attn_fixed_call(cs_tab, csp_tab, kmax_tab, qa, ka, vt, sag, gs, lamp, lam_init):
    B, _, S, _ = qa.shape
    grid = (B, N_HEADS, S // TQ)
    smem = pl.BlockSpec(memory_space=pltpu.SMEM)
    return pl.pallas_call(
        functools.partial(_attn_fixed_kernel, lam_init=lam_init),
        grid=grid,
        in_specs=[
            smem, smem, smem,
            pl.BlockSpec((None, None, TQ, LANES), lambda b, h, i: (b, 2 * h, i, 0)),
            pl.BlockSpec((None, None, TQ, LANES), lambda b, h, i: (b, 2 * h + 1, i, 0)),
            pl.BlockSpec((None, None, S, LANES), lambda b, h, i: (b, 2 * h, 0, 0)),
            pl.BlockSpec((None, None, S, LANES), lambda b, h, i: (b, 2 * h + 1, 0, 0)),
            pl.BlockSpec((None, None, V_DIM, S), lambda b, h, i: (b, h, 0, 0)),
            pl.BlockSpec((1, TQ, V_DIM), lambda b, h, i: (b, i, h)),
            pl.BlockSpec((V_DIM, 1), lambda b, h, i: (0, 0)),
            pl.BlockSpec((4, HEAD_DIM), lambda b, h, i: (0, 0)),
        ],
        out_specs=pl.BlockSpec((1, TQ, V_DIM), lambda b, h, i: (b, i, h)),
        out_shape=jax.ShapeDtypeStruct((B, S, ATTN_WIDTH), jnp.bfloat16),
        scratch_shapes=[
            pltpu.VMEM((2, 2, TQ, LANES), jnp.bfloat16),
            pltpu.VMEM((2, TK, TQ), jnp.bfloat16),
            pltpu.VMEM((2, TK, TQ), jnp.bfloat16),
            pltpu.VMEM((2, V_DIM, TQ), jnp.float32),
        ],
        compiler_params=pltpu.CompilerParams(
            dimension_semantics=("arbitrary", "arbitrary", "arbitrary"),
            vmem_limit_bytes=VMEM_LIMIT_BYTES),
        name="diff_attn_fixed",
    )(cs_tab, csp_tab, kmax_tab, qa, qa, ka, ka, vt, sag, gs, lamp)


def _merge_kernel(x_ref, og_ref, up_ref, uc_ref, un_ref, scg_ref, gpre_ref, wmg_ref,
                  wao_ref, wco_ref, wout_ref, wdw_ref, bdw_ref, gcn_ref, bcn_ref,
                  o_ref, ucat_ref):
    tm = x_ref.shape[1]
    i = pl.program_id(1)
    n_i = pl.num_programs(1)

    ucat_ref[0:HALO] = jnp.where(i > 0, up_ref[0], 0.0)
    ucat_ref[HALO:HALO + tm] = uc_ref[0]
    ucat_ref[HALO + tm:HALO + tm + HALO] = jnp.where(i < n_i - 1, un_ref[0], 0.0)
    y = jnp.zeros((tm, CONV_WIDTH), jnp.float32) + bdw_ref[...]
    for t in range(CONV_KERNEL):
        lo = HALO - CONV_PAD + t
        y = y + ucat_ref[lo:lo + tm] * wdw_ref[t:t + 1]
    mu = jnp.mean(y, axis=-1, keepdims=True)
    yc = y - mu
    yn = yc * lax.rsqrt(jnp.mean(yc * yc, axis=-1, keepdims=True) + EPS) * gcn_ref[...] + bcn_ref[...]
    ub = (_silu(yn) * scg_ref[0]).astype(jnp.bfloat16)

    y_a = jnp.dot(og_ref[0], wao_ref[...], preferred_element_type=jnp.float32)
    y_b = jnp.dot(ub, wco_ref[...], preferred_element_type=jnp.float32)

    x = x_ref[0]
    h = x * lax.rsqrt(jnp.mean(x * x, axis=-1, keepdims=True) + EPS) * gpre_ref[...]
    zmg = jnp.dot(h.astype(jnp.bfloat16), wmg_ref[...], preferred_element_type=jnp.float32)
    gates = jax.nn.sigmoid(zmg)
    m = gates[:, :D_MODEL] * y_a + gates[:, D_MODEL:] * y_b
    o_ref[0] = x + jnp.dot(m.astype(jnp.bfloat16), wout_ref[...], preferred_element_type=jnp.float32)


def _merge_call(x, og, u, scg, g_pre, w_mg, w_ao, w_co, w_out, w_dw, b_dw, g_cn, b_cn):
    B, S, _ = x.shape
    n_i = S // TM_MERGE
    grid = (B, n_i)
    hpb = TM_MERGE // HALO
    n_hb = S // HALO
    row_blk = lambda b, i: (b, i, 0)
    const2 = lambda b, i: (0, 0)
    return pl.pallas_call(
        _merge_kernel,
        grid=grid,
        in_specs=[
            pl.BlockSpec((1, TM_MERGE, D_MODEL), row_blk),
            pl.BlockSpec((1, TM_MERGE, ATTN_WIDTH), row_blk),
            pl.BlockSpec((1, HALO, CONV_WIDTH), lambda b, i: (b, jnp.maximum(i * hpb - 1, 0), 0)),
            pl.BlockSpec((1, TM_MERGE, CONV_WIDTH), row_blk),
            pl.BlockSpec((1, HALO, CONV_WIDTH), lambda b, i: (b, jnp.minimum((i + 1) * hpb, n_hb - 1), 0)),
            pl.BlockSpec((1, TM_MERGE, CONV_WIDTH), row_blk),
            pl.BlockSpec((1, D_MODEL), const2),
            pl.BlockSpec((D_MODEL, 2 * D_MODEL), const2),
            pl.BlockSpec((ATTN_WIDTH, D_MODEL), const2),
            pl.BlockSpec((CONV_WIDTH, D_MODEL), const2),
            pl.BlockSpec((D_MODEL, D_MODEL), const2),
            pl.BlockSpec((CONV_KERNEL, CONV_WIDTH), const2),
            pl.BlockSpec((1, CONV_WIDTH), const2),
            pl.BlockSpec((1, CONV_WIDTH), const2),
            pl.BlockSpec((1, CONV_WIDTH), const2),
        ],
        out_specs=pl.BlockSpec((1, TM_MERGE, D_MODEL), row_blk),
        out_shape=jax.ShapeDtypeStruct((B, S, D_MODEL), jnp.float32),
        scratch_shapes=[pltpu.VMEM((TM_MERGE + 2 * HALO, CONV_WIDTH), jnp.float32)],
        compiler_params=pltpu.CompilerParams(
            dimension_semantics=("arbitrary", "arbitrary"),
            vmem_limit_bytes=VMEM_LIMIT_BYTES),
        name="merge_out",
    )(x, og, u, u, u, scg, g_pre, w_mg, w_ao, w_co, w_out, w_dw, b_dw, g_cn, b_cn)


def _lambda_init(layer_idx):
    return 0.8 - 0.6 * math.exp(-0.3 * layer_idx)


def kernel(x, g_pre, w_in, g_q, g_k, lam_q1, lam_k1, lam_q2, lam_k2, g_subln, w_attn_out, w_dw, b_dw, g_cn, b_cn, w_conv_out, w_out):
    B, S, D = x.shape
    depth = w_in.shape[0]
    assert D == D_MODEL and w_in.shape[2] == D_IN
    assert S % TK == 0 and S % TM_PROJ == 0 and S % TM_MERGE == 0

    bf = jnp.bfloat16
    cs_tab = jnp.asarray([s * LOG2E for s in ALIBI_SLOPES], jnp.float32)
    csp_tab = jnp.asarray([p for s in ALIBI_SLOPES for p in _bf16_pieces_py(s * LOG2E)], jnp.float32)
    for l in range(depth):
        w_l = w_in[l].astype(bf)
        qa, ka, vt, sag, u, scg, qsq, ksq = _proj_call(
            x, g_pre[l][None, :], w_l[:, :OFF_MG],
            jnp.tile(g_q[l], 2)[None, :], jnp.tile(g_k[l], 2)[None, :])
        lamp = jnp.stack([lam_q1[l], lam_k1[l], lam_q2[l], lam_k2[l]]).astype(jnp.float32)
        gs = g_subln[l][:, None]
        lam_init = _lambda_init(l)
        qmax = jnp.sqrt(jnp.max(qsq[..., 0], axis=1))
        kmax = jnp.sqrt(jnp.max(ksq[..., 0], axis=1))
        fixed_ok = jnp.all(qmax * kmax < FIXED_SHIFT_MAX_QK)
        og = lax.cond(
            fixed_ok,
            lambda: _attn_fixed_call(cs_tab, csp_tab, kmax.reshape(-1), qa, ka, vt, sag, gs, lamp, lam_init),
            lambda: _attn_call(cs_tab, qa, ka, vt, sag, gs, lamp, lam_init))
        x = _merge_call(
            x, og, u, scg, g_pre[l][None, :], w_l[:, OFF_MG:],
            w_attn_out[l].astype(bf), w_conv_out[l].astype(bf), w_out[l].astype(bf),
            w_dw[l], b_dw[l][None, :], g_cn[l][None, :], b_cn[l][None, :])
    return x
```

```python
import functools
import math

import numpy as np
import jax
import jax.numpy as jnp
from jax import lax
from jax.experimental import pallas as pl
from jax.experimental.pallas import tpu as pltpu

D_MODEL = 1024
HEAD_DIM = 64
V_DIM = 2 * HEAD_DIM
ATTN_WIDTH = D_MODEL // 2
N_HEADS = ATTN_WIDTH // V_DIM
CONV_WIDTH = D_MODEL // 2
CONV_KERNEL = 31
CONV_PAD = (CONV_KERNEL - 1) // 2
EPS = 1e-6
ALIBI_SLOPES = tuple(2.0 ** (-8.0 * (i + 1) / N_HEADS) for i in range(N_HEADS))
LOG2E = math.log2(math.e)
Q_SCALE = LOG2E / math.sqrt(HEAD_DIM)

OFF_Q = 0
OFF_K = OFF_Q + N_HEADS * 2 * HEAD_DIM
OFF_V = OFF_K + N_HEADS * 2 * HEAD_DIM
OFF_AG = OFF_V + ATTN_WIDTH
OFF_GLU = OFF_AG + ATTN_WIDTH
OFF_CG = OFF_GLU + 2 * CONV_WIDTH
OFF_MG = OFF_CG + CONV_WIDTH
D_IN = OFF_MG + 2 * D_MODEL

LANES = 128
SUBLANES = 8
VMEM_LIMIT_BYTES = 56 * 1024 * 1024

TM_PROJ = 256
TM_MERGE = 256
TQ = 512
TK = 1024
SM_ROWS = 32
HALO = 16

POS_SPLIT = 128
MAX_SEQ = POS_SPLIT * 256
N_PIECES = 3
N_FLIP = 4 * N_PIECES
N_AUG = 5 * N_PIECES
NORM_SAFETY = 1.01
FIXED_SHIFT_MAX_QK = 45.0
NEG_INIT = -1e30

assert TK % TQ == 0 and HALO >= CONV_PAD and N_AUG <= HEAD_DIM


def _bf16_pieces(v, n=N_PIECES):
    out = []
    r = v
    for _ in range(n):
        p = r.astype(jnp.bfloat16).astype(jnp.float32)
        out.append(p)
        r = r - p
    return out


def _bf16_pieces_py(v, n=N_PIECES):
    out = []
    r = float(v)
    for _ in range(n):
        p = float(np.asarray(r, np.float32).astype(jnp.bfloat16).astype(np.float32))
        out.append(p)
        r = r - p
    return out


def _silu(x):
    return x * jax.nn.sigmoid(x)


def _proj_kernel(x_ref, gpre_ref, w_ref, gq_ref, gk_ref,
                 qa_ref, ka_ref, vt_ref, sag_ref, u_ref, scg_ref):
    tm = x_ref.shape[1]
    x = x_ref[0]
    h = x * lax.rsqrt(jnp.mean(x * x, axis=-1, keepdims=True) + EPS) * gpre_ref[...]
    h = h.astype(jnp.bfloat16)

    def proj(lo, width):
        return jnp.dot(h, w_ref[:, lo:lo + width], preferred_element_type=jnp.float32)

    lane = lax.broadcasted_iota(jnp.int32, (tm, LANES), 1)
    row = pl.program_id(1) * tm + lax.broadcasted_iota(jnp.int32, (tm, LANES), 0)
    lo_half = lane < HEAD_DIM
    aug_idx = lane & (HEAD_DIM - 1)
    grp = [(aug_idx >= N_PIECES * g) & (aug_idx < N_PIECES * (g + 1)) for g in range(5)]
    pos_bits = POS_SPLIT.bit_length() - 1
    pos_hi = (row >> pos_bits).astype(jnp.float32)
    pos_lo = (row & (POS_SPLIT - 1)).astype(jnp.float32)

    k_bound = (jnp.max(jnp.abs(gk_ref[...]), axis=-1, keepdims=True)
               * (math.sqrt(HEAD_DIM) * NORM_SAFETY))

    def half_sums(v):
        return (jnp.sum(jnp.where(lo_half, v, 0.0), axis=-1, keepdims=True),
                jnp.sum(jnp.where(lo_half, 0.0, v), axis=-1, keepdims=True))

    def qk_norm(z, g):
        s_lo, s_hi = half_sums(z * z)
        r = jnp.where(lo_half, lax.rsqrt(s_lo * (1.0 / HEAD_DIM) + EPS),
                      lax.rsqrt(s_hi * (1.0 / HEAD_DIM) + EPS))
        return z * r * g

    def put(base, g, vals):
        out = base
        for i, v in enumerate(vals):
            out = jnp.where(aug_idx == N_PIECES * g + i, v, out)
        return out

    zq = proj(OFF_Q, N_HEADS * 2 * HEAD_DIM)
    zk = proj(OFF_K, N_HEADS * 2 * HEAD_DIM)
    zero = jnp.zeros((tm, LANES), jnp.float32)
    for hh in range(N_HEADS):
        c = _bf16_pieces_py(ALIBI_SLOPES[hh] * LOG2E)
        sl = slice(hh * LANES, (hh + 1) * LANES)
        qn = qk_norm(zq[:, sl], gq_ref[...]) * Q_SCALE
        kn = qk_norm(zk[:, sl], gk_ref[...])

        aug_k = jnp.where(grp[0], pos_hi, jnp.where(grp[1], pos_lo, jnp.where(grp[4], 1.0, zero)))
        aug_k = put(put(aug_k, 2, [-POS_SPLIT * v for v in c]), 3, [-v for v in c])

        aug_q = jnp.where(grp[2], pos_hi, jnp.where(grp[3], pos_lo, zero))
        aug_q = put(put(aug_q, 0, [POS_SPLIT * v for v in c]), 1, c)
        n_lo, n_hi = half_sums(qn * qn)
        aug_q1 = put(aug_q, 4, _bf16_pieces(zero - jnp.sqrt(n_lo) * k_bound))
        aug_q2 = put(aug_q, 4, _bf16_pieces(zero - jnp.sqrt(n_hi) * k_bound))

        qa_ref[0, 2 * hh] = jnp.where(lo_half, qn, aug_q1).astype(jnp.bfloat16)
        qa_ref[0, 2 * hh + 1] = jnp.where(lo_half, aug_q2, qn).astype(jnp.bfloat16)
        ka_ref[0, 2 * hh] = jnp.where(lo_half, kn, aug_k).astype(jnp.bfloat16)
        ka_ref[0, 2 * hh + 1] = jnp.where(lo_half, aug_k, kn).astype(jnp.bfloat16)

    zv = proj(OFF_V, ATTN_WIDTH)
    for hh in range(N_HEADS):
        vt_ref[0, hh] = zv[:, hh * V_DIM:(hh + 1) * V_DIM].T.astype(jnp.bfloat16)

    sag_ref[0] = _silu(proj(OFF_AG, ATTN_WIDTH))
    zg = proj(OFF_GLU, 2 * CONV_WIDTH)
    u_ref[0] = zg[:, :CONV_WIDTH] * jax.nn.sigmoid(zg[:, CONV_WIDTH:])
    scg_ref[0] = _silu(proj(OFF_CG, CONV_WIDTH))


def _proj_call(x, g_pre, w_bf, g_q2, g_k2):
    B, S, _ = x.shape
    grid = (B, S // TM_PROJ)
    row_blk = lambda b, i: (b, i, 0)
    const2 = lambda b, i: (0, 0)
    out_shape = (
        jax.ShapeDtypeStruct((B, 2 * N_HEADS, S, LANES), jnp.bfloat16),
        jax.ShapeDtypeStruct((B, 2 * N_HEADS, S, LANES), jnp.bfloat16),
        jax.ShapeDtypeStruct((B, N_HEADS, V_DIM, S), jnp.bfloat16),
        jax.ShapeDtypeStruct((B, S, ATTN_WIDTH), jnp.float32),
        jax.ShapeDtypeStruct((B, S, CONV_WIDTH), jnp.float32),
        jax.ShapeDtypeStruct((B, S, CONV_WIDTH), jnp.float32),
    )
    return pl.pallas_call(
        _proj_kernel,
        grid=grid,
        in_specs=[
            pl.BlockSpec((1, TM_PROJ, D_MODEL), row_blk),
            pl.BlockSpec((1, D_MODEL), const2),
            pl.BlockSpec((D_MODEL, OFF_MG), const2),
            pl.BlockSpec((1, LANES), const2),
            pl.BlockSpec((1, LANES), const2),
        ],
        out_specs=(
            pl.BlockSpec((1, 2 * N_HEADS, TM_PROJ, LANES), lambda b, i: (b, 0, i, 0)),
            pl.BlockSpec((1, 2 * N_HEADS, TM_PROJ, LANES), lambda b, i: (b, 0, i, 0)),
            pl.BlockSpec((1, N_HEADS, V_DIM, TM_PROJ), lambda b, i: (b, 0, 0, i)),
            pl.BlockSpec((1, TM_PROJ, ATTN_WIDTH), row_blk),
            pl.BlockSpec((1, TM_PROJ, CONV_WIDTH), row_blk),
            pl.BlockSpec((1, TM_PROJ, CONV_WIDTH), row_blk),
        ),
        out_shape=out_shape,
        compiler_params=pltpu.CompilerParams(
            dimension_semantics=("arbitrary", "arbitrary"),
            vmem_limit_bytes=VMEM_LIMIT_BYTES),
        name="proj",
    )(x, g_pre, w_bf, g_q2, g_k2)


def _attn_prologue(q1_ref, q2_ref, qs_ref, relu_ref):
    lane = lax.broadcasted_iota(jnp.int32, (TQ, LANES), 1)
    flip_cols = (lane & (HEAD_DIM - 1)) < N_FLIP
    q1 = q1_ref[...]
    q2 = q2_ref[...]
    qs_ref[0, 0] = q1
    qs_ref[0, 1] = q2
    qs_ref[1, 0] = jnp.where(flip_cols & (lane >= HEAD_DIM), -q1, q1)
    qs_ref[1, 1] = jnp.where(flip_cols & (lane < HEAD_DIM), -q2, q2)

    @pl.when(pl.program_id(2) == 0)
    def _():
        d = (lax.broadcasted_iota(jnp.int32, (TK, TQ), 0)
             - lax.broadcasted_iota(jnp.int32, (TK, TQ), 1))
        for o in range(TK // TQ):
            relu_ref[o] = jnp.maximum(d - o * TQ, 0).astype(jnp.float32)


def _attn_epilogue(acc_ref, l, sag_ref, gs_ref, lamp_ref, o_ref, lam_init):
    lp = lamp_ref[...]
    lam = (jnp.exp(jnp.sum(lp[0:1] * lp[1:2], axis=-1, keepdims=True))
           - jnp.exp(jnp.sum(lp[2:3] * lp[3:4], axis=-1, keepdims=True)) + lam_init)
    o = acc_ref[0] / l[0] - lam * (acc_ref[1] / l[1])
    o = o * lax.rsqrt(jnp.mean(o * o, axis=0, keepdims=True) + EPS) * gs_ref[...]
    o = o * (1.0 - lam_init)
    o_ref[0] = (o.T * sag_ref[0]).astype(jnp.bfloat16)


def _key_start(t):
    return t * TK if isinstance(t, int) else pl.multiple_of(t * TK, TK)


_NT = (((1,), (1,)), ((), ()))


def _attn_kernel(cs_ref, q1_ref, q2_ref, k1_ref, k2_ref, vt_ref, sag_ref, gs_ref, lamp_ref,
                 o_ref, qs_ref, relu_ref, sa_ref, sb_ref, pa_ref, pb_ref, mla_ref, mlb_ref, acc_ref,
                 *, lam_init):
    k_refs = (k1_ref, k2_ref)
    s_refs = (sa_ref, sb_ref)
    p_refs = (pa_ref, pb_ref)
    ml_refs = (mla_ref, mlb_ref)
    n_kb = k1_ref.shape[0] // TK
    assert n_kb >= 2 and n_kb % 2 == 0
    cs = cs_ref[pl.program_id(1)]
    i0 = pl.program_id(2) * TQ
    kd = i0 // TK
    off_idx = (i0 - kd * TK) // TQ

    _attn_prologue(q1_ref, q2_ref, qs_ref, relu_ref)
    acc_ref[...] = jnp.zeros_like(acc_ref)

    def qk(t, slot):
        side = jnp.where(t > kd, 1, 0)
        j0 = _key_start(t)
        for c in range(2):
            s = lax.dot_general(k_refs[c][pl.ds(j0, TK), :], qs_ref[side, c], _NT,
                                preferred_element_type=jnp.float32)
            s_refs[slot][c] = s
            ml_refs[slot][c] = jnp.max(s, axis=0, keepdims=True)

    def fix_diag(t, slot):
        @pl.when(t == kd)
        def _():
            corr = (2.0 * cs) * relu_ref[off_idx]
            for c in range(2):
                s = s_refs[slot][c] - corr
                s_refs[slot][c] = s
                ml_refs[slot][c] = jnp.max(s, axis=0, keepdims=True)

    def sm(slot, m, l):
        m_out, l_out, a_out = [], [], []
        for c in range(2):
            m_new = jnp.maximum(m[c], ml_refs[slot][c])
            alpha = jnp.exp2(m[c] - m_new)
            lsum = jnp.zeros((SUBLANES, TQ), jnp.float32)
            for r in range(0, TK, SM_ROWS):
                pc = jnp.exp2(s_refs[slot][c, r:r + SM_ROWS, :] - m_new)
                for r8 in range(0, SM_ROWS, SUBLANES):
                    lsum = lsum + pc[r8:r8 + SUBLANES]
                p_refs[slot][c, r:r + SM_ROWS, :] = pc.astype(jnp.bfloat16)
            l_out.append(alpha * l[c] + jnp.sum(lsum, axis=0, keepdims=True))
            m_out.append(m_new)
            a_out.append(alpha)
        return tuple(m_out), tuple(l_out), tuple(a_out)

    def av(t, slot, alpha):
        vt = vt_ref[:, pl.ds(_key_start(t), TK)]
        for c in range(2):
            acc_ref[c] = alpha[c] * acc_ref[c] + jnp.dot(
                vt, p_refs[slot][c], preferred_element_type=jnp.float32)

    neg = jnp.full((1, TQ), NEG_INIT, jnp.float32)
    zero = jnp.zeros((1, TQ), jnp.float32)
    m, l = (neg, neg), (zero, zero)

    qk(0, 0)
    fix_diag(0, 0)
    qk(1, 1)
    m, l, alpha = sm(0, m, l)

    def half(t, slot, m, l, alpha):
        fix_diag(t - 1, 1 - slot)
        qk(t, slot)
        m, l, alpha_new = sm(1 - slot, m, l)
        av(t - 2, slot, alpha)
        return m, l, alpha_new

    def pair(g, carry):
        t = 2 + 2 * g
        carry = half(t, 0, *carry)
        return half(t + 1, 1, *carry)

    m, l, alpha = lax.fori_loop(0, (n_kb - 2) // 2, pair, (m, l, alpha))

    fix_diag(n_kb - 1, 1)
    m, l, alpha_last = sm(1, m, l)
    av(n_kb - 2, 0, alpha)
    av(n_kb - 1, 1, alpha_last)
    _attn_epilogue(acc_ref, l, sag_ref, gs_ref, lamp_ref, o_ref, lam_init)


def _attn_fixed_kernel(cs_ref, q1_ref, q2_ref, k1_ref, k2_ref, vt_ref, sag_ref, gs_ref, lamp_ref,
                       o_ref, qs_ref, relu_ref, pa_ref, pb_ref, acc_ref, *, lam_init):
    k_refs = (k1_ref, k2_ref)
    p_refs = (pa_ref, pb_ref)
    n_kb = k1_ref.shape[0] // TK
    assert n_kb >= 2 and n_kb % 2 == 0
    cs = cs_ref[pl.program_id(1)]
    i0 = pl.program_id(2) * TQ
    kd = i0 // TK
    off_idx = (i0 - kd * TK) // TQ

    _attn_prologue(q1_ref, q2_ref, qs_ref, relu_ref)
    acc_ref[...] = jnp.zeros_like(acc_ref)

    def stage_a(t, slot, lsum, diag):
        side = jnp.where(t > kd, 1, 0)
        j0 = _key_start(t)
        if diag:
            corr = (2.0 * cs) * relu_ref[off_idx]
        out = []
        for c in range(2):
            s = lax.dot_general(k_refs[c][pl.ds(j0, TK), :], qs_ref[side, c], _NT,
                                preferred_element_type=jnp.float32)
            if diag:
                s = s - corr
            ls = lsum[c]
            for r in range(0, TK, SM_ROWS):
                pc = jnp.exp2(s[r:r + SM_ROWS, :])
                for r8 in range(0, SM_ROWS, SUBLANES):
                    ls = ls + pc[r8:r8 + SUBLANES]
                p_refs[slot][c, r:r + SM_ROWS, :] = pc.astype(jnp.bfloat16)
            out.append(ls)
        return tuple(out)

    def stage_b(t, slot):
        vt = vt_ref[:, pl.ds(_key_start(t), TK)]
        for c in range(2):
            acc_ref[c] += jnp.dot(vt, p_refs[slot][c], preferred_element_type=jnp.float32)

    def nth(n):
        return (n - 1) + jnp.where(n - 1 >= kd, 1, 0)

    zero = jnp.zeros((SUBLANES, TQ), jnp.float32)
    lsum = stage_a(kd, 0, (zero, zero), True)
    lsum = stage_a(nth(1), 1, lsum, False)
    stage_b(kd, 0)

    def pair(g, lsum):
        n = 2 + 2 * g
        lsum = stage_a(nth(n), 0, lsum, False)
        stage_b(nth(n - 1), 1)
        lsum = stage_a(nth(n + 1), 1, lsum, False)
        stage_b(nth(n), 0)
        return lsum

    lsum = lax.fori_loop(0, (n_kb - 2) // 2, pair, lsum)
    stage_b(nth(n_kb - 1), 1)
    l = [jnp.sum(ls, axis=0, keepdims=True) for ls in lsum]
    _attn_epilogue(acc_ref, l, sag_ref, gs_ref, lamp_ref, o_ref, lam_init)


def _attn_call(fixed_shift, cs_tab, qa, ka, vt, sag, gs, lamp, lam_init):
    B, _, S, _ = qa.shape
    grid = (B, N_HEADS, S // TQ)
    common = [
        pltpu.VMEM((2, 2, TQ, LANES), jnp.bfloat16),
        pltpu.VMEM((TK // TQ, TK, TQ), jnp.float32),
    ]
    tile = lambda dt: pltpu.VMEM((2, TK, TQ), dt)
    acc = pltpu.VMEM((2, V_DIM, TQ), jnp.float32)
    if fixed_shift:
        body, name = _attn_fixed_kernel, "diff_attn_fixed"
        scratch = common + [tile(jnp.bfloat16), tile(jnp.bfloat16), acc]
    else:
        body, name = _attn_kernel, "diff_attn"
        col_max = pltpu.VMEM((2, 1, TQ), jnp.float32)
        scratch = common + [tile(jnp.float32), tile(jnp.float32), tile(jnp.bfloat16),
                            tile(jnp.bfloat16), col_max, col_max, acc]
    return pl.pallas_call(
        functools.partial(body, lam_init=lam_init),
        grid=grid,
        in_specs=[
            pl.BlockSpec(memory_space=pltpu.SMEM),
            pl.BlockSpec((None, None, TQ, LANES), lambda b, h, i: (b, 2 * h, i, 0)),
            pl.BlockSpec((None, None, TQ, LANES), lambda b, h, i: (b, 2 * h + 1, i, 0)),
            pl.BlockSpec((None, None, S, LANES), lambda b, h, i: (b, 2 * h, 0, 0)),
            pl.BlockSpec((None, None, S, LANES), lambda b, h, i: (b, 2 * h + 1, 0, 0)),
            pl.BlockSpec((None, None, V_DIM, S), lambda b, h, i: (b, h, 0, 0)),
            pl.BlockSpec((1, TQ, V_DIM), lambda b, h, i: (b, i, h)),
            pl.BlockSpec((V_DIM, 1), lambda b, h, i: (0, 0)),
            pl.BlockSpec((4, HEAD_DIM), lambda b, h, i: (0, 0)),
        ],
        out_specs=pl.BlockSpec((1, TQ, V_DIM), lambda b, h, i: (b, i, h)),
        out_shape=jax.ShapeDtypeStruct((B, S, ATTN_WIDTH), jnp.bfloat16),
        scratch_shapes=scratch,
        compiler_params=pltpu.CompilerParams(
            dimension_semantics=("arbitrary", "arbitrary", "arbitrary"),
            vmem_limit_bytes=VMEM_LIMIT_BYTES),
        name=name,
    )(cs_tab, qa, qa, ka, ka, vt, sag, gs, lamp)


def _merge_kernel(x_ref, og_ref, up_ref, uc_ref, un_ref, scg_ref, gpre_ref, wmg_ref,
                  wao_ref, wco_ref, wout_ref, wdw_ref, bdw_ref, gcn_ref, bcn_ref,
                  o_ref, ucat_ref):
    tm = x_ref.shape[1]
    i = pl.program_id(1)
    n_i = pl.num_programs(1)

    ucat_ref[0:HALO] = jnp.where(i > 0, up_ref[0], 0.0)
    ucat_ref[HALO:HALO + tm] = uc_ref[0]
    ucat_ref[HALO + tm:HALO + tm + HALO] = jnp.where(i < n_i - 1, un_ref[0], 0.0)
    y = jnp.zeros((tm, CONV_WIDTH), jnp.float32) + bdw_ref[...]
    for t in range(CONV_KERNEL):
        lo = HALO - CONV_PAD + t
        y = y + ucat_ref[lo:lo + tm] * wdw_ref[t:t + 1]
    mu = jnp.mean(y, axis=-1, keepdims=True)
    yc = y - mu
    yn = yc * lax.rsqrt(jnp.mean(yc * yc, axis=-1, keepdims=True) + EPS) * gcn_ref[...] + bcn_ref[...]
    ub = (_silu(yn) * scg_ref[0]).astype(jnp.bfloat16)

    y_a = jnp.dot(og_ref[0], wao_ref[...], preferred_element_type=jnp.float32)
    y_b = jnp.dot(ub, wco_ref[...], preferred_element_type=jnp.float32)

    x = x_ref[0]
    h = x * lax.rsqrt(jnp.mean(x * x, axis=-1, keepdims=True) + EPS) * gpre_ref[...]
    zmg = jnp.dot(h.astype(jnp.bfloat16), wmg_ref[...], preferred_element_type=jnp.float32)
    gates = jax.nn.sigmoid(zmg)
    m = gates[:, :D_MODEL] * y_a + gates[:, D_MODEL:] * y_b
    o_ref[0] = x + jnp.dot(m.astype(jnp.bfloat16), wout_ref[...], preferred_element_type=jnp.float32)


def _merge_call(x, og, u, scg, g_pre, w_mg, w_ao, w_co, w_out, w_dw, b_dw, g_cn, b_cn):
    B, S, _ = x.shape
    n_i = S // TM_MERGE
    grid = (B, n_i)
    hpb = TM_MERGE // HALO
    n_hb = S // HALO
    row_blk = lambda b, i: (b, i, 0)
    const2 = lambda b, i: (0, 0)
    return pl.pallas_call(
        _merge_kernel,
        grid=grid,
        in_specs=[
            pl.BlockSpec((1, TM_MERGE, D_MODEL), row_blk),
            pl.BlockSpec((1, TM_MERGE, ATTN_WIDTH), row_blk),
            pl.BlockSpec((1, HALO, CONV_WIDTH), lambda b, i: (b, jnp.maximum(i * hpb - 1, 0), 0)),
            pl.BlockSpec((1, TM_MERGE, CONV_WIDTH), row_blk),
            pl.BlockSpec((1, HALO, CONV_WIDTH), lambda b, i: (b, jnp.minimum((i + 1) * hpb, n_hb - 1), 0)),
            pl.BlockSpec((1, TM_MERGE, CONV_WIDTH), row_blk),
            pl.BlockSpec((1, D_MODEL), const2),
            pl.BlockSpec((D_MODEL, 2 * D_MODEL), const2),
            pl.BlockSpec((ATTN_WIDTH, D_MODEL), const2),
            pl.BlockSpec((CONV_WIDTH, D_MODEL), const2),
            pl.BlockSpec((D_MODEL, D_MODEL), const2),
            pl.BlockSpec((CONV_KERNEL, CONV_WIDTH), const2),
            pl.BlockSpec((1, CONV_WIDTH), const2),
            pl.BlockSpec((1, CONV_WIDTH), const2),
            pl.BlockSpec((1, CONV_WIDTH), const2),
        ],
        out_specs=pl.BlockSpec((1, TM_MERGE, D_MODEL), row_blk),
        out_shape=jax.ShapeDtypeStruct((B, S, D_MODEL), jnp.float32),
        scratch_shapes=[pltpu.VMEM((TM_MERGE + 2 * HALO, CONV_WIDTH), jnp.float32)],
        compiler_params=pltpu.CompilerParams(
            dimension_semantics=("arbitrary", "arbitrary"),
            vmem_limit_bytes=VMEM_LIMIT_BYTES),
        name="merge_out",
    )(x, og, u, u, u, scg, g_pre, w_mg, w_ao, w_co, w_out, w_dw, b_dw, g_cn, b_cn)


def _lambda_init(layer_idx):
    return 0.8 - 0.6 * math.exp(-0.3 * layer_idx)


def kernel(x, g_pre, w_in, g_q, g_k, lam_q1, lam_k1, lam_q2, lam_k2, g_subln, w_attn_out, w_dw, b_dw, g_cn, b_cn, w_conv_out, w_out):
    B, S, D = x.shape
    depth = w_in.shape[0]
    assert D == D_MODEL and w_in.shape[2] == D_IN and S <= MAX_SEQ
    assert S % TK == 0 and S % TM_PROJ == 0 and S % TM_MERGE == 0

    bf = jnp.bfloat16
    cs_tab = jnp.asarray([s * LOG2E for s in ALIBI_SLOPES], jnp.float32)
    for l in range(depth):
        w_l = w_in[l].astype(bf)
        qa, ka, vt, sag, u, scg = _proj_call(
            x, g_pre[l][None, :], w_l[:, :OFF_MG],
            jnp.tile(g_q[l], 2)[None, :], jnp.tile(g_k[l], 2)[None, :])
        lamp = jnp.stack([lam_q1[l], lam_k1[l], lam_q2[l], lam_k2[l]]).astype(jnp.float32)
        attn = functools.partial(_attn_call, cs_tab=cs_tab, qa=qa, ka=ka, vt=vt, sag=sag,
                                 gs=g_subln[l][:, None], lamp=lamp, lam_init=_lambda_init(l))
        norm_bound = math.sqrt(HEAD_DIM) * NORM_SAFETY
        qk_bound = (norm_bound * Q_SCALE * jnp.max(jnp.abs(g_q[l]))) * (norm_bound * jnp.max(jnp.abs(g_k[l])))
        og = lax.cond(qk_bound < FIXED_SHIFT_MAX_QK,
                      lambda: attn(True), lambda: attn(False))
        x = _merge_call(
            x, og, u, scg, g_pre[l][None, :], w_l[:, OFF_MG:],
            w_attn_out[l].astype(bf), w_conv_out[l].astype(bf), w_out[l].astype(bf),
            w_dw[l], b_dw[l][None, :], g_cn[l][None, :], b_cn[l][None, :])
    return x
```

```python
import functools
import math

import numpy as np
import jax
import jax.numpy as jnp
from jax import lax
from jax.experimental import pallas as pl
from jax.experimental.pallas import tpu as pltpu

D_MODEL = 1024
HEAD_DIM = 64
V_DIM = 2 * HEAD_DIM
ATTN_WIDTH = D_MODEL // 2
N_HEADS = ATTN_WIDTH // V_DIM
CONV_WIDTH = D_MODEL // 2
CONV_KERNEL = 31
CONV_PAD = (CONV_KERNEL - 1) // 2
EPS = 1e-6
ALIBI_SLOPES = tuple(2.0 ** (-8.0 * (i + 1) / N_HEADS) for i in range(N_HEADS))
LOG2E = math.log2(math.e)
Q_SCALE = LOG2E / math.sqrt(HEAD_DIM)

OFF_Q = 0
OFF_K = OFF_Q + N_HEADS * 2 * HEAD_DIM
OFF_V = OFF_K + N_HEADS * 2 * HEAD_DIM
OFF_AG = OFF_V + ATTN_WIDTH
OFF_GLU = OFF_AG + ATTN_WIDTH
OFF_CG = OFF_GLU + 2 * CONV_WIDTH
OFF_MG = OFF_CG + CONV_WIDTH
D_IN = OFF_MG + 2 * D_MODEL

LANES = 128
SUBLANES = 8
VMEM_LIMIT_BYTES = 56 * 1024 * 1024

TM_PROJ = 256
TM_MERGE = 256
TQ = 512
TK = 1024
SM_ROWS = 32
HALO = 16
CONV_ROWS = 32
CONV_SHIFT0 = HALO - CONV_PAD
CONV_SPAN = ((CONV_SHIFT0 + CONV_KERNEL - 1) // SUBLANES) * SUBLANES

POS_SPLIT = 128
MAX_SEQ = POS_SPLIT * 256
N_PIECES = 3
N_FLIP = 4 * N_PIECES
N_AUG = 5 * N_PIECES
NORM_SAFETY = 1.01
FIXED_SHIFT_MAX_QK = 45.0
NEG_INIT = -1e30

assert TK % TQ == 0 and HALO >= CONV_PAD and N_AUG <= HEAD_DIM
assert TM_MERGE % CONV_ROWS == 0 and CONV_ROWS % SUBLANES == 0


def _bf16_pieces(v, n=N_PIECES):
    out = []
    r = v
    for _ in range(n):
        p = r.astype(jnp.bfloat16).astype(jnp.float32)
        out.append(p)
        r = r - p
    return out


def _bf16_pieces_py(v, n=N_PIECES):
    out = []
    r = float(v)
    for _ in range(n):
        p = float(np.asarray(r, np.float32).astype(jnp.bfloat16).astype(np.float32))
        out.append(p)
        r = r - p
    return out


def _silu(x):
    return x * jax.nn.sigmoid(x)


def _proj_kernel(x_ref, gpre_ref, w_ref, gq_ref, gk_ref,
                 qa_ref, ka_ref, vt_ref, sag_ref, u_ref, scg_ref):
    tm = x_ref.shape[1]
    x = x_ref[0]
    h = x * lax.rsqrt(jnp.mean(x * x, axis=-1, keepdims=True) + EPS) * gpre_ref[...]
    h = h.astype(jnp.bfloat16)

    def proj(lo, width):
        return jnp.dot(h, w_ref[:, lo:lo + width], preferred_element_type=jnp.float32)

    lane = lax.broadcasted_iota(jnp.int32, (tm, LANES), 1)
    row = pl.program_id(1) * tm + lax.broadcasted_iota(jnp.int32, (tm, LANES), 0)
    lo_half = lane < HEAD_DIM
    aug_idx = lane & (HEAD_DIM - 1)
    grp = [(aug_idx >= N_PIECES * g) & (aug_idx < N_PIECES * (g + 1)) for g in range(5)]
    pos_bits = POS_SPLIT.bit_length() - 1
    pos_hi = (row >> pos_bits).astype(jnp.float32)
    pos_lo = (row & (POS_SPLIT - 1)).astype(jnp.float32)

    k_bound = (jnp.max(jnp.abs(gk_ref[...]), axis=-1, keepdims=True)
               * (math.sqrt(HEAD_DIM) * NORM_SAFETY))

    def half_sums(v):
        return (jnp.sum(jnp.where(lo_half, v, 0.0), axis=-1, keepdims=True),
                jnp.sum(jnp.where(lo_half, 0.0, v), axis=-1, keepdims=True))

    def qk_norm(z, g):
        s_lo, s_hi = half_sums(z * z)
        r = jnp.where(lo_half, lax.rsqrt(s_lo * (1.0 / HEAD_DIM) + EPS),
                      lax.rsqrt(s_hi * (1.0 / HEAD_DIM) + EPS))
        return z * r * g

    def put(base, g, vals):
        out = base
        for i, v in enumerate(vals):
            out = jnp.where(aug_idx == N_PIECES * g + i, v, out)
        return out

    zq = proj(OFF_Q, N_HEADS * 2 * HEAD_DIM)
    zk = proj(OFF_K, N_HEADS * 2 * HEAD_DIM)
    zero = jnp.zeros((tm, LANES), jnp.float32)
    for hh in range(N_HEADS):
        c = _bf16_pieces_py(ALIBI_SLOPES[hh] * LOG2E)
        sl = slice(hh * LANES, (hh + 1) * LANES)
        qn = qk_norm(zq[:, sl], gq_ref[...]) * Q_SCALE
        kn = qk_norm(zk[:, sl], gk_ref[...])

        aug_k = jnp.where(grp[0], pos_hi, jnp.where(grp[1], pos_lo, jnp.where(grp[4], 1.0, zero)))
        aug_k = put(put(aug_k, 2, [-POS_SPLIT * v for v in c]), 3, [-v for v in c])

        aug_q = jnp.where(grp[2], pos_hi, jnp.where(grp[3], pos_lo, zero))
        aug_q = put(put(aug_q, 0, [POS_SPLIT * v for v in c]), 1, c)
        n_lo, n_hi = half_sums(qn * qn)
        aug_q1 = put(aug_q, 4, _bf16_pieces(zero - jnp.sqrt(n_lo) * k_bound))
        aug_q2 = put(aug_q, 4, _bf16_pieces(zero - jnp.sqrt(n_hi) * k_bound))

        qa_ref[0, 2 * hh] = jnp.where(lo_half, qn, aug_q1).astype(jnp.bfloat16)
        qa_ref[0, 2 * hh + 1] = jnp.where(lo_half, aug_q2, qn).astype(jnp.bfloat16)
        ka_ref[0, 2 * hh] = jnp.where(lo_half, kn, aug_k).astype(jnp.bfloat16)
        ka_ref[0, 2 * hh + 1] = jnp.where(lo_half, aug_k, kn).astype(jnp.bfloat16)

    zv = proj(OFF_V, ATTN_WIDTH)
    for hh in range(N_HEADS):
        vt_ref[0, hh] = zv[:, hh * V_DIM:(hh + 1) * V_DIM].T.astype(jnp.bfloat16)

    sag_ref[0] = _silu(proj(OFF_AG, ATTN_WIDTH))
    zg = proj(OFF_GLU, 2 * CONV_WIDTH)
    u_ref[0] = zg[:, :CONV_WIDTH] * jax.nn.sigmoid(zg[:, CONV_WIDTH:])
    scg_ref[0] = _silu(proj(OFF_CG, CONV_WIDTH))


def _proj_call(x, g_pre, w_bf, g_q2, g_k2):
    B, S, _ = x.shape
    grid = (B, S // TM_PROJ)
    row_blk = lambda b, i: (b, i, 0)
    const2 = lambda b, i: (0, 0)
    out_shape = (
        jax.ShapeDtypeStruct((B, 2 * N_HEADS, S, LANES), jnp.bfloat16),
        jax.ShapeDtypeStruct((B, 2 * N_HEADS, S, LANES), jnp.bfloat16),
        jax.ShapeDtypeStruct((B, N_HEADS, V_DIM, S), jnp.bfloat16),
        jax.ShapeDtypeStruct((B, S, ATTN_WIDTH), jnp.float32),
        jax.ShapeDtypeStruct((B, S, CONV_WIDTH), jnp.float32),
        jax.ShapeDtypeStruct((B, S, CONV_WIDTH), jnp.float32),
    )
    return pl.pallas_call(
        _proj_kernel,
        grid=grid,
        in_specs=[
            pl.BlockSpec((1, TM_PROJ, D_MODEL), row_blk),
            pl.BlockSpec((1, D_MODEL), const2),
            pl.BlockSpec((D_MODEL, OFF_MG), const2),
            pl.BlockSpec((1, LANES), const2),
            pl.BlockSpec((1, LANES), const2),
        ],
        out_specs=(
            pl.BlockSpec((1, 2 * N_HEADS, TM_PROJ, LANES), lambda b, i: (b, 0, i, 0)),
            pl.BlockSpec((1, 2 * N_HEADS, TM_PROJ, LANES), lambda b, i: (b, 0, i, 0)),
            pl.BlockSpec((1, N_HEADS, V_DIM, TM_PROJ), lambda b, i: (b, 0, 0, i)),
            pl.BlockSpec((1, TM_PROJ, ATTN_WIDTH), row_blk),
            pl.BlockSpec((1, TM_PROJ, CONV_WIDTH), row_blk),
            pl.BlockSpec((1, TM_PROJ, CONV_WIDTH), row_blk),
        ),
        out_shape=out_shape,
        compiler_params=pltpu.CompilerParams(
            dimension_semantics=("arbitrary", "arbitrary"),
            vmem_limit_bytes=VMEM_LIMIT_BYTES),
        name="proj",
    )(x, g_pre, w_bf, g_q2, g_k2)


def _attn_prologue(q1_ref, q2_ref, qs_ref, relu_ref):
    lane = lax.broadcasted_iota(jnp.int32, (TQ, LANES), 1)
    flip_cols = (lane & (HEAD_DIM - 1)) < N_FLIP
    q1 = q1_ref[...]
    q2 = q2_ref[...]
    qs_ref[0, 0] = q1
    qs_ref[0, 1] = q2
    qs_ref[1, 0] = jnp.where(flip_cols & (lane >= HEAD_DIM), -q1, q1)
    qs_ref[1, 1] = jnp.where(flip_cols & (lane < HEAD_DIM), -q2, q2)

    @pl.when(pl.program_id(2) == 0)
    def _():
        d = (lax.broadcasted_iota(jnp.int32, (TK, TQ), 0)
             - lax.broadcasted_iota(jnp.int32, (TK, TQ), 1))
        for o in range(TK // TQ):
            relu_ref[o] = jnp.maximum(d - o * TQ, 0).astype(jnp.float32)


def _attn_epilogue(acc_ref, l, sag_ref, gs_ref, lamp_ref, o_ref, lam_init):
    lp = lamp_ref[...]
    lam = (jnp.exp(jnp.sum(lp[0:1] * lp[1:2], axis=-1, keepdims=True))
           - jnp.exp(jnp.sum(lp[2:3] * lp[3:4], axis=-1, keepdims=True)) + lam_init)
    o = acc_ref[0] / l[0] - lam * (acc_ref[1] / l[1])
    o = o * lax.rsqrt(jnp.mean(o * o, axis=0, keepdims=True) + EPS) * gs_ref[...]
    o = o * (1.0 - lam_init)
    o_ref[0] = (o.T * sag_ref[0]).astype(jnp.bfloat16)


def _key_start(t):
    return t * TK if isinstance(t, int) else pl.multiple_of(t * TK, TK)


_NT = (((1,), (1,)), ((), ()))


def _attn_kernel(cs_ref, q1_ref, q2_ref, k1_ref, k2_ref, vt_ref, sag_ref, gs_ref, lamp_ref,
                 o_ref, qs_ref, relu_ref, sa_ref, sb_ref, pa_ref, pb_ref, mla_ref, mlb_ref, acc_ref,
                 *, lam_init):
    k_refs = (k1_ref, k2_ref)
    s_refs = (sa_ref, sb_ref)
    p_refs = (pa_ref, pb_ref)
    ml_refs = (mla_ref, mlb_ref)
    n_kb = k1_ref.shape[0] // TK
    assert n_kb >= 2 and n_kb % 2 == 0
    cs = cs_ref[pl.program_id(1)]
    i0 = pl.program_id(2) * TQ
    kd = i0 // TK
    off_idx = (i0 - kd * TK) // TQ

    _attn_prologue(q1_ref, q2_ref, qs_ref, relu_ref)
    acc_ref[...] = jnp.zeros_like(acc_ref)

    def qk(t, slot):
        side = jnp.where(t > kd, 1, 0)
        j0 = _key_start(t)
        for c in range(2):
            s = lax.dot_general(k_refs[c][pl.ds(j0, TK), :], qs_ref[side, c], _NT,
                                preferred_element_type=jnp.float32)
            s_refs[slot][c] = s
            ml_refs[slot][c] = jnp.max(s, axis=0, keepdims=True)

    def fix_diag(t, slot):
        @pl.when(t == kd)
        def _():
            corr = (2.0 * cs) * relu_ref[off_idx]
            for c in range(2):
                s = s_refs[slot][c] - corr
                s_refs[slot][c] = s
                ml_refs[slot][c] = jnp.max(s, axis=0, keepdims=True)

    def sm(slot, m, l):
        m_out, l_out, a_out = [], [], []
        for c in range(2):
            m_new = jnp.maximum(m[c], ml_refs[slot][c])
            alpha = jnp.exp2(m[c] - m_new)
            lsum = jnp.zeros((SUBLANES, TQ), jnp.float32)
            for r in range(0, TK, SM_ROWS):
                pc = jnp.exp2(s_refs[slot][c, r:r + SM_ROWS, :] - m_new)
                for r8 in range(0, SM_ROWS, SUBLANES):
                    lsum = lsum + pc[r8:r8 + SUBLANES]
                p_refs[slot][c, r:r + SM_ROWS, :] = pc.astype(jnp.bfloat16)
            l_out.append(alpha * l[c] + jnp.sum(lsum, axis=0, keepdims=True))
            m_out.append(m_new)
            a_out.append(alpha)
        return tuple(m_out), tuple(l_out), tuple(a_out)

    def av(t, slot, alpha):
        vt = vt_ref[:, pl.ds(_key_start(t), TK)]
        for c in range(2):
            acc_ref[c] = alpha[c] * acc_ref[c] + jnp.dot(
                vt, p_refs[slot][c], preferred_element_type=jnp.float32)

    neg = jnp.full((1, TQ), NEG_INIT, jnp.float32)
    zero = jnp.zeros((1, TQ), jnp.float32)
    m, l = (neg, neg), (zero, zero)

    qk(0, 0)
    fix_diag(0, 0)
    qk(1, 1)
    m, l, alpha = sm(0, m, l)

    def half(t, slot, m, l, alpha):
        fix_diag(t - 1, 1 - slot)
        qk(t, slot)
        m, l, alpha_new = sm(1 - slot, m, l)
        av(t - 2, slot, alpha)
        return m, l, alpha_new

    def pair(g, carry):
        t = 2 + 2 * g
        carry = half(t, 0, *carry)
        return half(t + 1, 1, *carry)

    m, l, alpha = lax.fori_loop(0, (n_kb - 2) // 2, pair, (m, l, alpha))

    fix_diag(n_kb - 1, 1)
    m, l, alpha_last = sm(1, m, l)
    av(n_kb - 2, 0, alpha)
    av(n_kb - 1, 1, alpha_last)
    _attn_epilogue(acc_ref, l, sag_ref, gs_ref, lamp_ref, o_ref, lam_init)


def _attn_fixed_kernel(cs_ref, q1_ref, q2_ref, k1_ref, k2_ref, vt_ref, sag_ref, gs_ref, lamp_ref,
                       o_ref, qs_ref, relu_ref, pa_ref, pb_ref, acc_ref, *, lam_init):
    k_refs = (k1_ref, k2_ref)
    p_refs = (pa_ref, pb_ref)
    n_kb = k1_ref.shape[0] // TK
    assert n_kb >= 2 and n_kb % 2 == 0
    cs = cs_ref[pl.program_id(1)]
    i0 = pl.program_id(2) * TQ
    kd = i0 // TK
    off_idx = (i0 - kd * TK) // TQ

    _attn_prologue(q1_ref, q2_ref, qs_ref, relu_ref)
    acc_ref[...] = jnp.zeros_like(acc_ref)

    def stage_a(t, slot, lsum, diag):
        side = jnp.where(t > kd, 1, 0)
        j0 = _key_start(t)
        if diag:
            corr = (2.0 * cs) * relu_ref[off_idx]
        out = []
        for c in range(2):
            s = lax.dot_general(k_refs[c][pl.ds(j0, TK), :], qs_ref[side, c], _NT,
                                preferred_element_type=jnp.float32)
            if diag:
                s = s - corr
            ls = lsum[c]
            for r in range(0, TK, SM_ROWS):
                pc = jnp.exp2(s[r:r + SM_ROWS, :])
                for r8 in range(0, SM_ROWS, SUBLANES):
                    ls = ls + pc[r8:r8 + SUBLANES]
                p_refs[slot][c, r:r + SM_ROWS, :] = pc.astype(jnp.bfloat16)
            out.append(ls)
        return tuple(out)

    def stage_b(t, slot):
        vt = vt_ref[:, pl.ds(_key_start(t), TK)]
        for c in range(2):
            acc_ref[c] += jnp.dot(vt, p_refs[slot][c], preferred_element_type=jnp.float32)

    def nth(n):
        return (n - 1) + jnp.where(n - 1 >= kd, 1, 0)

    zero = jnp.zeros((SUBLANES, TQ), jnp.float32)
    lsum = stage_a(kd, 0, (zero, zero), True)
    lsum = stage_a(nth(1), 1, lsum, False)
    stage_b(kd, 0)

    def pair(g, lsum):
        n = 2 + 2 * g
        lsum = stage_a(nth(n), 0, lsum, False)
        stage_b(nth(n - 1), 1)
        lsum = stage_a(nth(n + 1), 1, lsum, False)
        stage_b(nth(n), 0)
        return lsum

    lsum = lax.fori_loop(0, (n_kb - 2) // 2, pair, lsum)
    stage_b(nth(n_kb - 1), 1)
    l = [jnp.sum(ls, axis=0, keepdims=True) for ls in lsum]
    _attn_epilogue(acc_ref, l, sag_ref, gs_ref, lamp_ref, o_ref, lam_init)


def _attn_call(fixed_shift, cs_tab, qa, ka, vt, sag, gs, lamp, lam_init):
    B, _, S, _ = qa.shape
    grid = (B, N_HEADS, S // TQ)
    common = [
        pltpu.VMEM((2, 2, TQ, LANES), jnp.bfloat16),
        pltpu.VMEM((TK // TQ, TK, TQ), jnp.float32),
    ]
    tile = lambda dt: pltpu.VMEM((2, TK, TQ), dt)
    acc = pltpu.VMEM((2, V_DIM, TQ), jnp.float32)
    if fixed_shift:
        body, name = _attn_fixed_kernel, "diff_attn_fixed"
        scratch = common + [tile(jnp.bfloat16), tile(jnp.bfloat16), acc]
    else:
        body, name = _attn_kernel, "diff_attn"
        col_max = pltpu.VMEM((2, 1, TQ), jnp.float32)
        scratch = common + [tile(jnp.float32), tile(jnp.float32), tile(jnp.bfloat16),
                            tile(jnp.bfloat16), col_max, col_max, acc]
    return pl.pallas_call(
        functools.partial(body, lam_init=lam_init),
        grid=grid,
        in_specs=[
            pl.BlockSpec(memory_space=pltpu.SMEM),
            pl.BlockSpec((None, None, TQ, LANES), lambda b, h, i: (b, 2 * h, i, 0)),
            pl.BlockSpec((None, None, TQ, LANES), lambda b, h, i: (b, 2 * h + 1, i, 0)),
            pl.BlockSpec((None, None, S, LANES), lambda b, h, i: (b, 2 * h, 0, 0)),
            pl.BlockSpec((None, None, S, LANES), lambda b, h, i: (b, 2 * h + 1, 0, 0)),
            pl.BlockSpec((None, None, V_DIM, S), lambda b, h, i: (b, h, 0, 0)),
            pl.BlockSpec((1, TQ, V_DIM), lambda b, h, i: (b, i, h)),
            pl.BlockSpec((V_DIM, 1), lambda b, h, i: (0, 0)),
            pl.BlockSpec((4, HEAD_DIM), lambda b, h, i: (0, 0)),
        ],
        out_specs=pl.BlockSpec((1, TQ, V_DIM), lambda b, h, i: (b, i, h)),
        out_shape=jax.ShapeDtypeStruct((B, S, ATTN_WIDTH), jnp.bfloat16),
        scratch_shapes=scratch,
        compiler_params=pltpu.CompilerParams(
            dimension_semantics=("arbitrary", "arbitrary", "arbitrary"),
            vmem_limit_bytes=VMEM_LIMIT_BYTES),
        name=name,
    )(cs_tab, qa, qa, ka, ka, vt, sag, gs, lamp)


def _merge_kernel(x_ref, og_ref, up_ref, uc_ref, un_ref, scg_ref, gpre_ref, wmg_ref,
                  wao_ref, wco_ref, wout_ref, wdw_ref, bdw_ref, gcn_ref, bcn_ref,
                  o_ref, ucat_ref, ush_ref, ub_ref):
    tm = x_ref.shape[1]
    i = pl.program_id(1)
    n_i = pl.num_programs(1)

    ucat_ref[0:HALO] = jnp.where(i > 0, up_ref[0], 0.0)
    ucat_ref[HALO:HALO + tm] = uc_ref[0]
    ucat_ref[HALO + tm:HALO + tm + HALO] = jnp.where(i < n_i - 1, un_ref[0], 0.0)

    for rho in range(1, SUBLANES):
        ush_ref[rho] = ucat_ref[rho:rho + tm + CONV_SPAN, :]

    wdw = wdw_ref[...]
    for r0 in range(0, tm, CONV_ROWS):
        y = jnp.zeros((CONV_ROWS, CONV_WIDTH), jnp.float32) + bdw_ref[...]
        for t in range(CONV_KERNEL):
            rho = (CONV_SHIFT0 + t) % SUBLANES
            a = r0 + CONV_SHIFT0 + t - rho
            src = ucat_ref[a:a + CONV_ROWS] if rho == 0 else ush_ref[rho, a:a + CONV_ROWS]
            y = y + src * wdw[t:t + 1]
        mu = jnp.mean(y, axis=-1, keepdims=True)
        yc = y - mu
        yn = (yc * lax.rsqrt(jnp.mean(yc * yc, axis=-1, keepdims=True) + EPS) * gcn_ref[...]
              + bcn_ref[...])
        ub_ref[r0:r0 + CONV_ROWS] = (_silu(yn) * scg_ref[0, r0:r0 + CONV_ROWS]).astype(jnp.bfloat16)

    y_a = jnp.dot(og_ref[0], wao_ref[...], preferred_element_type=jnp.float32)
    y_b = jnp.dot(ub_ref[...], wco_ref[...], preferred_element_type=jnp.float32)

    x = x_ref[0]
    h = x * lax.rsqrt(jnp.mean(x * x, axis=-1, keepdims=True) + EPS) * gpre_ref[...]
    zmg = jnp.dot(h.astype(jnp.bfloat16), wmg_ref[...], preferred_element_type=jnp.float32)
    gates = jax.nn.sigmoid(zmg)
    m = gates[:, :D_MODEL] * y_a + gates[:, D_MODEL:] * y_b
    o_ref[0] = x + jnp.dot(m.astype(jnp.bfloat16), wout_ref[...], preferred_element_type=jnp.float32)


def _merge_call(x, og, u, scg, g_pre, w_mg, w_ao, w_co, w_out, w_dw, b_dw, g_cn, b_cn):
    B, S, _ = x.shape
    n_i = S // TM_MERGE
    grid = (B, n_i)
    hpb = TM_MERGE // HALO
    n_hb = S // HALO
    row_blk = lambda b, i: (b, i, 0)
    const2 = lambda b, i: (0, 0)
    return pl.pallas_call(
        _merge_kernel,
        grid=grid,
        in_specs=[
            pl.BlockSpec((1, TM_MERGE, D_MODEL), row_blk),
            pl.BlockSpec((1, TM_MERGE, ATTN_WIDTH), row_blk),
            pl.BlockSpec((1, HALO, CONV_WIDTH), lambda b, i: (b, jnp.maximum(i * hpb - 1, 0), 0)),
            pl.BlockSpec((1, TM_MERGE, CONV_WIDTH), row_blk),
            pl.BlockSpec((1, HALO, CONV_WIDTH), lambda b, i: (b, jnp.minimum((i + 1) * hpb, n_hb - 1), 0)),
            pl.BlockSpec((1, TM_MERGE, CONV_WIDTH), row_blk),
            pl.BlockSpec((1, D_MODEL), const2),
            pl.BlockSpec((D_MODEL, 2 * D_MODEL), const2),
            pl.BlockSpec((ATTN_WIDTH, D_MODEL), const2),
            pl.BlockSpec((CONV_WIDTH, D_MODEL), const2),
            pl.BlockSpec((D_MODEL, D_MODEL), const2),
            pl.BlockSpec((CONV_KERNEL, CONV_WIDTH), const2),
            pl.BlockSpec((1, CONV_WIDTH), const2),
            pl.BlockSpec((1, CONV_WIDTH), const2),
            pl.BlockSpec((1, CONV_WIDTH), const2),
        ],
        out_specs=pl.BlockSpec((1, TM_MERGE, D_MODEL), row_blk),
        out_shape=jax.ShapeDtypeStruct((B, S, D_MODEL), jnp.float32),
        scratch_shapes=[pltpu.VMEM((TM_MERGE + 2 * HALO, CONV_WIDTH), jnp.float32),
                        pltpu.VMEM((SUBLANES, TM_MERGE + CONV_SPAN, CONV_WIDTH), jnp.float32),
                        pltpu.VMEM((TM_MERGE, CONV_WIDTH), jnp.bfloat16)],
        compiler_params=pltpu.CompilerParams(
            dimension_semantics=("arbitrary", "arbitrary"),
            vmem_limit_bytes=VMEM_LIMIT_BYTES),
        name="merge_out",
    )(x, og, u, u, u, scg, g_pre, w_mg, w_ao, w_co, w_out, w_dw, b_dw, g_cn, b_cn)


def _lambda_init(layer_idx):
    return 0.8 - 0.6 * math.exp(-0.3 * layer_idx)


def kernel(x, g_pre, w_in, g_q, g_k, lam_q1, lam_k1, lam_q2, lam_k2, g_subln, w_attn_out, w_dw, b_dw, g_cn, b_cn, w_conv_out, w_out):
    B, S, D = x.shape
    depth = w_in.shape[0]
    assert D == D_MODEL and w_in.shape[2] == D_IN and S <= MAX_SEQ
    assert S % TK == 0 and S % TM_PROJ == 0 and S % TM_MERGE == 0

    bf = jnp.bfloat16
    cs_tab = jnp.asarray([s * LOG2E for s in ALIBI_SLOPES], jnp.float32)
    for l in range(depth):
        w_l = w_in[l].astype(bf)
        qa, ka, vt, sag, u, scg = _proj_call(
            x, g_pre[l][None, :], w_l[:, :OFF_MG],
            jnp.tile(g_q[l], 2)[None, :], jnp.tile(g_k[l], 2)[None, :])
        lamp = jnp.stack([lam_q1[l], lam_k1[l], lam_q2[l], lam_k2[l]]).astype(jnp.float32)
        attn = functools.partial(_attn_call, cs_tab=cs_tab, qa=qa, ka=ka, vt=vt, sag=sag,
                                 gs=g_subln[l][:, None], lamp=lamp, lam_init=_lambda_init(l))
        norm_bound = math.sqrt(HEAD_DIM) * NORM_SAFETY
        qk_bound = (norm_bound * Q_SCALE * jnp.max(jnp.abs(g_q[l]))) * (norm_bound * jnp.max(jnp.abs(g_k[l])))
        og = lax.cond(qk_bound < FIXED_SHIFT_MAX_QK,
                      lambda: attn(True), lambda: attn(False))
        x = _merge_call(
            x, og, u, scg, g_pre[l][None, :], w_l[:, OFF_MG:],
            w_attn_out[l].astype(bf), w_conv_out[l].astype(bf), w_out[l].astype(bf),
            w_dw[l], b_dw[l][None, :], g_cn[l][None, :], b_cn[l][None, :])
    return x
```

```python
import functools
import math

import numpy as np
import jax
import jax.numpy as jnp
from jax import lax
from jax.experimental import pallas as pl
from jax.experimental.pallas import tpu as pltpu

D_MODEL = 1024
HEAD_DIM = 64
V_DIM = 2 * HEAD_DIM
ATTN_WIDTH = D_MODEL // 2
N_HEADS = ATTN_WIDTH // V_DIM
CONV_WIDTH = D_MODEL // 2
CONV_KERNEL = 31
CONV_PAD = (CONV_KERNEL - 1) // 2
EPS = 1e-6
ALIBI_SLOPES = tuple(2.0 ** (-8.0 * (i + 1) / N_HEADS) for i in range(N_HEADS))
LOG2E = math.log2(math.e)
Q_SCALE = LOG2E / math.sqrt(HEAD_DIM)

OFF_Q = 0
OFF_K = OFF_Q + N_HEADS * 2 * HEAD_DIM
OFF_V = OFF_K + N_HEADS * 2 * HEAD_DIM
OFF_AG = OFF_V + ATTN_WIDTH
OFF_GLU = OFF_AG + ATTN_WIDTH
OFF_CG = OFF_GLU + 2 * CONV_WIDTH
OFF_MG = OFF_CG + CONV_WIDTH
D_IN = OFF_MG + 2 * D_MODEL

LANES = 128
SUBLANES = 8
VMEM_LIMIT_BYTES = 56 * 1024 * 1024

TM_PROJ = 256
TM_MERGE = 256
TQ_FIXED = 1024
TQ_RUNNING = 512
TK = 1024
SM_ROWS = 16
HALO = 16
CONV_ROWS = 32
CONV_SHIFT0 = HALO - CONV_PAD
CONV_SPAN = ((CONV_SHIFT0 + CONV_KERNEL - 1) // SUBLANES) * SUBLANES

POS_SPLIT = 128
MAX_SEQ = POS_SPLIT * 256
N_PIECES = 3
N_FLIP = 4 * N_PIECES
N_AUG = 5 * N_PIECES
NORM_SAFETY = 1.01
FIXED_SHIFT_MAX_QK = 45.0
NEG_INIT = -1e30

assert TK % TQ_FIXED == 0 and TK % TQ_RUNNING == 0 and HALO >= CONV_PAD and N_AUG <= HEAD_DIM
assert TM_MERGE % CONV_ROWS == 0 and CONV_ROWS % SUBLANES == 0


def _bf16_pieces(v, n=N_PIECES):
    out = []
    r = v
    for _ in range(n):
        p = r.astype(jnp.bfloat16).astype(jnp.float32)
        out.append(p)
        r = r - p
    return out


def _bf16_pieces_py(v, n=N_PIECES):
    out = []
    r = float(v)
    for _ in range(n):
        p = float(np.asarray(r, np.float32).astype(jnp.bfloat16).astype(np.float32))
        out.append(p)
        r = r - p
    return out


def _silu(x):
    return x * jax.nn.sigmoid(x)


def _proj_kernel(x_ref, gpre_ref, w_ref, gq_ref, gk_ref,
                 qa_ref, ka_ref, vt_ref, sag_ref, u_ref, scg_ref):
    tm = x_ref.shape[1]
    x = x_ref[0]
    h = x * lax.rsqrt(jnp.mean(x * x, axis=-1, keepdims=True) + EPS) * gpre_ref[...]
    h = h.astype(jnp.bfloat16)

    def proj(lo, width):
        return jnp.dot(h, w_ref[:, lo:lo + width], preferred_element_type=jnp.float32)

    lane = lax.broadcasted_iota(jnp.int32, (tm, LANES), 1)
    row = pl.program_id(1) * tm + lax.broadcasted_iota(jnp.int32, (tm, LANES), 0)
    lo_half = lane < HEAD_DIM
    aug_idx = lane & (HEAD_DIM - 1)
    grp = [(aug_idx >= N_PIECES * g) & (aug_idx < N_PIECES * (g + 1)) for g in range(5)]
    pos_bits = POS_SPLIT.bit_length() - 1
    pos_hi = (row >> pos_bits).astype(jnp.float32)
    pos_lo = (row & (POS_SPLIT - 1)).astype(jnp.float32)

    k_bound = (jnp.max(jnp.abs(gk_ref[...]), axis=-1, keepdims=True)
               * (math.sqrt(HEAD_DIM) * NORM_SAFETY))

    def half_sums(v):
        return (jnp.sum(jnp.where(lo_half, v, 0.0), axis=-1, keepdims=True),
                jnp.sum(jnp.where(lo_half, 0.0, v), axis=-1, keepdims=True))

    def qk_norm(z, g):
        s_lo, s_hi = half_sums(z * z)
        r = jnp.where(lo_half, lax.rsqrt(s_lo * (1.0 / HEAD_DIM) + EPS),
                      lax.rsqrt(s_hi * (1.0 / HEAD_DIM) + EPS))
        return z * r * g

    def put(base, g, vals):
        out = base
        for i, v in enumerate(vals):
            out = jnp.where(aug_idx == N_PIECES * g + i, v, out)
        return out

    zq = proj(OFF_Q, N_HEADS * 2 * HEAD_DIM)
    zk = proj(OFF_K, N_HEADS * 2 * HEAD_DIM)
    zero = jnp.zeros((tm, LANES), jnp.float32)
    for hh in range(N_HEADS):
        c = _bf16_pieces_py(ALIBI_SLOPES[hh] * LOG2E)
        sl = slice(hh * LANES, (hh + 1) * LANES)
        qn = qk_norm(zq[:, sl], gq_ref[...]) * Q_SCALE
        kn = qk_norm(zk[:, sl], gk_ref[...])

        aug_k = jnp.where(grp[0], pos_hi, jnp.where(grp[1], pos_lo, jnp.where(grp[4], 1.0, zero)))
        aug_k = put(put(aug_k, 2, [-POS_SPLIT * v for v in c]), 3, [-v for v in c])

        aug_q = jnp.where(grp[2], pos_hi, jnp.where(grp[3], pos_lo, zero))
        aug_q = put(put(aug_q, 0, [POS_SPLIT * v for v in c]), 1, c)
        n_lo, n_hi = half_sums(qn * qn)
        aug_q1 = put(aug_q, 4, _bf16_pieces(zero - jnp.sqrt(n_lo) * k_bound))
        aug_q2 = put(aug_q, 4, _bf16_pieces(zero - jnp.sqrt(n_hi) * k_bound))

        qa_ref[0, 2 * hh] = jnp.where(lo_half, qn, aug_q1).astype(jnp.bfloat16)
        qa_ref[0, 2 * hh + 1] = jnp.where(lo_half, aug_q2, qn).astype(jnp.bfloat16)
        ka_ref[0, 2 * hh] = jnp.where(lo_half, kn, aug_k).astype(jnp.bfloat16)
        ka_ref[0, 2 * hh + 1] = jnp.where(lo_half, aug_k, kn).astype(jnp.bfloat16)

    zv = proj(OFF_V, ATTN_WIDTH)
    for hh in range(N_HEADS):
        vt_ref[0, hh] = zv[:, hh * V_DIM:(hh + 1) * V_DIM].T.astype(jnp.bfloat16)

    sag_ref[0] = _silu(proj(OFF_AG, ATTN_WIDTH))
    zg = proj(OFF_GLU, 2 * CONV_WIDTH)
    u_ref[0] = zg[:, :CONV_WIDTH] * jax.nn.sigmoid(zg[:, CONV_WIDTH:])
    scg_ref[0] = _silu(proj(OFF_CG, CONV_WIDTH))


def _proj_call(x, g_pre, w_bf, g_q2, g_k2):
    B, S, _ = x.shape
    grid = (B, S // TM_PROJ)
    row_blk = lambda b, i: (b, i, 0)
    const2 = lambda b, i: (0, 0)
    out_shape = (
        jax.ShapeDtypeStruct((B, 2 * N_HEADS, S, LANES), jnp.bfloat16),
        jax.ShapeDtypeStruct((B, 2 * N_HEADS, S, LANES), jnp.bfloat16),
        jax.ShapeDtypeStruct((B, N_HEADS, V_DIM, S), jnp.bfloat16),
        jax.ShapeDtypeStruct((B, S, ATTN_WIDTH), jnp.float32),
        jax.ShapeDtypeStruct((B, S, CONV_WIDTH), jnp.float32),
        jax.ShapeDtypeStruct((B, S, CONV_WIDTH), jnp.float32),
    )
    return pl.pallas_call(
        _proj_kernel,
        grid=grid,
        in_specs=[
            pl.BlockSpec((1, TM_PROJ, D_MODEL), row_blk),
            pl.BlockSpec((1, D_MODEL), const2),
            pl.BlockSpec((D_MODEL, OFF_MG), const2),
            pl.BlockSpec((1, LANES), const2),
            pl.BlockSpec((1, LANES), const2),
        ],
        out_specs=(
            pl.BlockSpec((1, 2 * N_HEADS, TM_PROJ, LANES), lambda b, i: (b, 0, i, 0)),
            pl.BlockSpec((1, 2 * N_HEADS, TM_PROJ, LANES), lambda b, i: (b, 0, i, 0)),
            pl.BlockSpec((1, N_HEADS, V_DIM, TM_PROJ), lambda b, i: (b, 0, 0, i)),
            pl.BlockSpec((1, TM_PROJ, ATTN_WIDTH), row_blk),
            pl.BlockSpec((1, TM_PROJ, CONV_WIDTH), row_blk),
            pl.BlockSpec((1, TM_PROJ, CONV_WIDTH), row_blk),
        ),
        out_shape=out_shape,
        compiler_params=pltpu.CompilerParams(
            dimension_semantics=("arbitrary", "arbitrary"),
            vmem_limit_bytes=VMEM_LIMIT_BYTES),
        name="proj",
    )(x, g_pre, w_bf, g_q2, g_k2)


def _attn_prologue(q1_ref, q2_ref, qs_ref, relu_ref):
    tq = q1_ref.shape[0]
    lane = lax.broadcasted_iota(jnp.int32, (tq, LANES), 1)
    flip_cols = (lane & (HEAD_DIM - 1)) < N_FLIP
    q1 = q1_ref[...]
    q2 = q2_ref[...]
    qs_ref[0, 0] = q1
    qs_ref[0, 1] = q2
    qs_ref[1, 0] = jnp.where(flip_cols & (lane >= HEAD_DIM), -q1, q1)
    qs_ref[1, 1] = jnp.where(flip_cols & (lane < HEAD_DIM), -q2, q2)

    @pl.when(pl.program_id(2) == 0)
    def _():
        d = (lax.broadcasted_iota(jnp.int32, (TK, tq), 0)
             - lax.broadcasted_iota(jnp.int32, (TK, tq), 1))
        for o in range(TK // tq):
            relu_ref[o] = jnp.maximum(d - o * tq, 0).astype(jnp.float32)


def _attn_epilogue(acc_ref, l, sag_ref, gs_ref, lamp_ref, o_ref, lam_init):
    lp = lamp_ref[...]
    lam = (jnp.exp(jnp.sum(lp[0:1] * lp[1:2], axis=-1, keepdims=True))
           - jnp.exp(jnp.sum(lp[2:3] * lp[3:4], axis=-1, keepdims=True)) + lam_init)
    o = acc_ref[0] / l[0] - lam * (acc_ref[1] / l[1])
    o = o * lax.rsqrt(jnp.mean(o * o, axis=0, keepdims=True) + EPS) * gs_ref[...]
    o = o * (1.0 - lam_init)
    o_ref[0] = (o.T * sag_ref[0]).astype(jnp.bfloat16)


def _key_start(t):
    return t * TK if isinstance(t, int) else pl.multiple_of(t * TK, TK)


_NT = (((1,), (1,)), ((), ()))


def _attn_kernel(cs_ref, q1_ref, q2_ref, k1_ref, k2_ref, vt_ref, sag_ref, gs_ref, lamp_ref,
                 o_ref, qs_ref, relu_ref, sa_ref, sb_ref, pa_ref, pb_ref, mla_ref, mlb_ref, acc_ref,
                 *, lam_init):
    k_refs = (k1_ref, k2_ref)
    s_refs = (sa_ref, sb_ref)
    p_refs = (pa_ref, pb_ref)
    ml_refs = (mla_ref, mlb_ref)
    n_kb = k1_ref.shape[0] // TK
    assert n_kb >= 2 and n_kb % 2 == 0
    tq = q1_ref.shape[0]
    cs = cs_ref[pl.program_id(1)]
    i0 = pl.program_id(2) * tq
    kd = i0 // TK
    off_idx = (i0 - kd * TK) // tq

    _attn_prologue(q1_ref, q2_ref, qs_ref, relu_ref)
    acc_ref[...] = jnp.zeros_like(acc_ref)

    def qk(t, slot):
        side = jnp.where(t > kd, 1, 0)
        j0 = _key_start(t)
        for c in range(2):
            s = lax.dot_general(k_refs[c][pl.ds(j0, TK), :], qs_ref[side, c], _NT,
                                preferred_element_type=jnp.float32)
            s_refs[slot][c] = s
            ml_refs[slot][c] = jnp.max(s, axis=0, keepdims=True)

    def fix_diag(t, slot):
        @pl.when(t == kd)
        def _():
            corr = (2.0 * cs) * relu_ref[off_idx]
            for c in range(2):
                s = s_refs[slot][c] - corr
                s_refs[slot][c] = s
                ml_refs[slot][c] = jnp.max(s, axis=0, keepdims=True)

    def sm(slot, m, l):
        m_out, l_out, a_out = [], [], []
        for c in range(2):
            m_new = jnp.maximum(m[c], ml_refs[slot][c])
            alpha = jnp.exp2(m[c] - m_new)
            lsum = jnp.zeros((SUBLANES, tq), jnp.float32)
            for r in range(0, TK, SM_ROWS):
                pc = jnp.exp2(s_refs[slot][c, r:r + SM_ROWS, :] - m_new)
                for r8 in range(0, SM_ROWS, SUBLANES):
                    lsum = lsum + pc[r8:r8 + SUBLANES]
                p_refs[slot][c, r:r + SM_ROWS, :] = pc.astype(jnp.bfloat16)
            l_out.append(alpha * l[c] + jnp.sum(lsum, axis=0, keepdims=True))
            m_out.append(m_new)
            a_out.append(alpha)
        return tuple(m_out), tuple(l_out), tuple(a_out)

    def av(t, slot, alpha):
        vt = vt_ref[:, pl.ds(_key_start(t), TK)]
        for c in range(2):
            acc_ref[c] = alpha[c] * acc_ref[c] + jnp.dot(
                vt, p_refs[slot][c], preferred_element_type=jnp.float32)

    neg = jnp.full((1, tq), NEG_INIT, jnp.float32)
    zero = jnp.zeros((1, tq), jnp.float32)
    m, l = (neg, neg), (zero, zero)

    qk(0, 0)
    fix_diag(0, 0)
    qk(1, 1)
    m, l, alpha = sm(0, m, l)

    def half(t, slot, m, l, alpha):
        fix_diag(t - 1, 1 - slot)
        qk(t, slot)
        m, l, alpha_new = sm(1 - slot, m, l)
        av(t - 2, slot, alpha)
        return m, l, alpha_new

    def pair(g, carry):
        t = 2 + 2 * g
        carry = half(t, 0, *carry)
        return half(t + 1, 1, *carry)

    m, l, alpha = lax.fori_loop(0, (n_kb - 2) // 2, pair, (m, l, alpha))

    fix_diag(n_kb - 1, 1)
    m, l, alpha_last = sm(1, m, l)
    av(n_kb - 2, 0, alpha)
    av(n_kb - 1, 1, alpha_last)
    _attn_epilogue(acc_ref, l, sag_ref, gs_ref, lamp_ref, o_ref, lam_init)


def _attn_fixed_kernel(cs_ref, q1_ref, q2_ref, k1_ref, k2_ref, vt_ref, sag_ref, gs_ref, lamp_ref,
                       o_ref, qs_ref, relu_ref, pa_ref, pb_ref, acc_ref, *, lam_init):
    k_refs = (k1_ref, k2_ref)
    p_refs = (pa_ref, pb_ref)
    n_kb = k1_ref.shape[0] // TK
    assert n_kb >= 2 and n_kb % 2 == 0
    tq = q1_ref.shape[0]
    cs = cs_ref[pl.program_id(1)]
    i0 = pl.program_id(2) * tq
    kd = i0 // TK
    off_idx = (i0 - kd * TK) // tq

    _attn_prologue(q1_ref, q2_ref, qs_ref, relu_ref)
    acc_ref[...] = jnp.zeros_like(acc_ref)

    def stage_a(t, slot, lsum, diag):
        side = jnp.where(t > kd, 1, 0)
        j0 = _key_start(t)
        if diag:
            corr = (2.0 * cs) * relu_ref[off_idx]
        out = []
        for c in range(2):
            s = lax.dot_general(k_refs[c][pl.ds(j0, TK), :], qs_ref[side, c], _NT,
                                preferred_element_type=jnp.float32)
            if diag:
                s = s - corr
            ls = lsum[c]
            for r in range(0, TK, SM_ROWS):
                pc = jnp.exp2(s[r:r + SM_ROWS, :])
                for r8 in range(0, SM_ROWS, SUBLANES):
                    ls = ls + pc[r8:r8 + SUBLANES]
                p_refs[slot][c, r:r + SM_ROWS, :] = pc.astype(jnp.bfloat16)
            out.append(ls)
        return tuple(out)

    def stage_b(t, slot):
        vt = vt_ref[:, pl.ds(_key_start(t), TK)]
        for c in range(2):
            acc_ref[c] += jnp.dot(vt, p_refs[slot][c], preferred_element_type=jnp.float32)

    def nth(n):
        return (n - 1) + jnp.where(n - 1 >= kd, 1, 0)

    zero = jnp.zeros((SUBLANES, tq), jnp.float32)
    lsum = stage_a(kd, 0, (zero, zero), True)
    lsum = stage_a(nth(1), 1, lsum, False)
    stage_b(kd, 0)

    def pair(g, lsum):
        n = 2 + 2 * g
        lsum = stage_a(nth(n), 0, lsum, False)
        stage_b(nth(n - 1), 1)
        lsum = stage_a(nth(n + 1), 1, lsum, False)
        stage_b(nth(n), 0)
        return lsum

    lsum = lax.fori_loop(0, (n_kb - 2) // 2, pair, lsum)
    stage_b(nth(n_kb - 1), 1)
    l = [jnp.sum(ls, axis=0, keepdims=True) for ls in lsum]
    _attn_epilogue(acc_ref, l, sag_ref, gs_ref, lamp_ref, o_ref, lam_init)


def _attn_call(fixed_shift, cs_tab, qa, ka, vt, sag, gs, lamp, lam_init):
    B, _, S, _ = qa.shape
    tq = TQ_FIXED if fixed_shift else TQ_RUNNING
    grid = (B, N_HEADS, S // tq)
    common = [
        pltpu.VMEM((2, 2, tq, LANES), jnp.bfloat16),
        pltpu.VMEM((TK // tq, TK, tq), jnp.float32),
    ]
    tile = lambda dt: pltpu.VMEM((2, TK, tq), dt)
    acc = pltpu.VMEM((2, V_DIM, tq), jnp.float32)
    if fixed_shift:
        body, name = _attn_fixed_kernel, "diff_attn_fixed"
        scratch = common + [tile(jnp.bfloat16), tile(jnp.bfloat16), acc]
    else:
        body, name = _attn_kernel, "diff_attn"
        col_max = pltpu.VMEM((2, 1, tq), jnp.float32)
        scratch = common + [tile(jnp.float32), tile(jnp.float32), tile(jnp.bfloat16),
                            tile(jnp.bfloat16), col_max, col_max, acc]
    return pl.pallas_call(
        functools.partial(body, lam_init=lam_init),
        grid=grid,
        in_specs=[
            pl.BlockSpec(memory_space=pltpu.SMEM),
            pl.BlockSpec((None, None, tq, LANES), lambda b, h, i: (b, 2 * h, i, 0)),
            pl.BlockSpec((None, None, tq, LANES), lambda b, h, i: (b, 2 * h + 1, i, 0)),
            pl.BlockSpec((None, None, S, LANES), lambda b, h, i: (b, 2 * h, 0, 0)),
            pl.BlockSpec((None, None, S, LANES), lambda b, h, i: (b, 2 * h + 1, 0, 0)),
            pl.BlockSpec((None, None, V_DIM, S), lambda b, h, i: (b, h, 0, 0)),
            pl.BlockSpec((1, tq, V_DIM), lambda b, h, i: (b, i, h)),
            pl.BlockSpec((V_DIM, 1), lambda b, h, i: (0, 0)),
            pl.BlockSpec((4, HEAD_DIM), lambda b, h, i: (0, 0)),
        ],
        out_specs=pl.BlockSpec((1, tq, V_DIM), lambda b, h, i: (b, i, h)),
        out_shape=jax.ShapeDtypeStruct((B, S, ATTN_WIDTH), jnp.bfloat16),
        scratch_shapes=scratch,
        compiler_params=pltpu.CompilerParams(
            dimension_semantics=("arbitrary", "arbitrary", "arbitrary"),
            vmem_limit_bytes=VMEM_LIMIT_BYTES),
        name=name,
    )(cs_tab, qa, qa, ka, ka, vt, sag, gs, lamp)


def _merge_kernel(x_ref, og_ref, up_ref, uc_ref, un_ref, scg_ref, gpre_ref, wmg_ref,
                  wao_ref, wco_ref, wout_ref, wdw_ref, bdw_ref, gcn_ref, bcn_ref,
                  o_ref, ucat_ref, ush_ref, ub_ref):
    tm = x_ref.shape[1]
    i = pl.program_id(1)
    n_i = pl.num_programs(1)

    ucat_ref[0:HALO] = jnp.where(i > 0, up_ref[0], 0.0)
    ucat_ref[HALO:HALO + tm] = uc_ref[0]
    ucat_ref[HALO + tm:HALO + tm + HALO] = jnp.where(i < n_i - 1, un_ref[0], 0.0)

    for rho in range(1, SUBLANES):
        ush_ref[rho] = ucat_ref[rho:rho + tm + CONV_SPAN, :]

    wdw = wdw_ref[...]
    for r0 in range(0, tm, CONV_ROWS):
        y = jnp.zeros((CONV_ROWS, CONV_WIDTH), jnp.float32) + bdw_ref[...]
        for t in range(CONV_KERNEL):
            rho = (CONV_SHIFT0 + t) % SUBLANES
            a = r0 + CONV_SHIFT0 + t - rho
            src = ucat_ref[a:a + CONV_ROWS] if rho == 0 else ush_ref[rho, a:a + CONV_ROWS]
            y = y + src * wdw[t:t + 1]
        mu = jnp.mean(y, axis=-1, keepdims=True)
        yc = y - mu
        yn = (yc * lax.rsqrt(jnp.mean(yc * yc, axis=-1, keepdims=True) + EPS) * gcn_ref[...]
              + bcn_ref[...])
        ub_ref[r0:r0 + CONV_ROWS] = (_silu(yn) * scg_ref[0, r0:r0 + CONV_ROWS]).astype(jnp.bfloat16)

    y_a = jnp.dot(og_ref[0], wao_ref[...], preferred_element_type=jnp.float32)
    y_b = jnp.dot(ub_ref[...], wco_ref[...], preferred_element_type=jnp.float32)

    x = x_ref[0]
    h = x * lax.rsqrt(jnp.mean(x * x, axis=-1, keepdims=True) + EPS) * gpre_ref[...]
    zmg = jnp.dot(h.astype(jnp.bfloat16), wmg_ref[...], preferred_element_type=jnp.float32)
    gates = jax.nn.sigmoid(zmg)
    m = gates[:, :D_MODEL] * y_a + gates[:, D_MODEL:] * y_b
    o_ref[0] = x + jnp.dot(m.astype(jnp.bfloat16), wout_ref[...], preferred_element_type=jnp.float32)


def _merge_call(x, og, u, scg, g_pre, w_mg, w_ao, w_co, w_out, w_dw, b_dw, g_cn, b_cn):
    B, S, _ = x.shape
    n_i = S // TM_MERGE
    grid = (B, n_i)
    hpb = TM_MERGE // HALO
    n_hb = S // HALO
    row_blk = lambda b, i: (b, i, 0)
    const2 = lambda b, i: (0, 0)
    return pl.pallas_call(
        _merge_kernel,
        grid=grid,
        in_specs=[
            pl.BlockSpec((1, TM_MERGE, D_MODEL), row_blk),
            pl.BlockSpec((1, TM_MERGE, ATTN_WIDTH), row_blk),
            pl.BlockSpec((1, HALO, CONV_WIDTH), lambda b, i: (b, jnp.maximum(i * hpb - 1, 0), 0)),
            pl.BlockSpec((1, TM_MERGE, CONV_WIDTH), row_blk),
            pl.BlockSpec((1, HALO, CONV_WIDTH), lambda b, i: (b, jnp.minimum((i + 1) * hpb, n_hb - 1), 0)),
            pl.BlockSpec((1, TM_MERGE, CONV_WIDTH), row_blk),
            pl.BlockSpec((1, D_MODEL), const2),
            pl.BlockSpec((D_MODEL, 2 * D_MODEL), const2),
            pl.BlockSpec((ATTN_WIDTH, D_MODEL), const2),
            pl.BlockSpec((CONV_WIDTH, D_MODEL), const2),
            pl.BlockSpec((D_MODEL, D_MODEL), const2),
            pl.BlockSpec((CONV_KERNEL, CONV_WIDTH), const2),
            pl.BlockSpec((1, CONV_WIDTH), const2),
            pl.BlockSpec((1, CONV_WIDTH), const2),
            pl.BlockSpec((1, CONV_WIDTH), const2),
        ],
        out_specs=pl.BlockSpec((1, TM_MERGE, D_MODEL), row_blk),
        out_shape=jax.ShapeDtypeStruct((B, S, D_MODEL), jnp.float32),
        scratch_shapes=[pltpu.VMEM((TM_MERGE + 2 * HALO, CONV_WIDTH), jnp.float32),
                        pltpu.VMEM((SUBLANES, TM_MERGE + CONV_SPAN, CONV_WIDTH), jnp.float32),
                        pltpu.VMEM((TM_MERGE, CONV_WIDTH), jnp.bfloat16)],
        compiler_params=pltpu.CompilerParams(
            dimension_semantics=("arbitrary", "arbitrary"),
            vmem_limit_bytes=VMEM_LIMIT_BYTES),
        name="merge_out",
    )(x, og, u, u, u, scg, g_pre, w_mg, w_ao, w_co, w_out, w_dw, b_dw, g_cn, b_cn)


def _lambda_init(layer_idx):
    return 0.8 - 0.6 * math.exp(-0.3 * layer_idx)


def kernel(x, g_pre, w_in, g_q, g_k, lam_q1, lam_k1, lam_q2, lam_k2, g_subln, w_attn_out, w_dw, b_dw, g_cn, b_cn, w_conv_out, w_out):
    B, S, D = x.shape
    depth = w_in.shape[0]
    assert D == D_MODEL and w_in.shape[2] == D_IN and S <= MAX_SEQ
    assert S % TK == 0 and S % TQ_FIXED == 0 and S % TM_PROJ == 0 and S % TM_MERGE == 0

    bf = jnp.bfloat16
    cs_tab = jnp.asarray([s * LOG2E for s in ALIBI_SLOPES], jnp.float32)
    for l in range(depth):
        w_l = w_in[l].astype(bf)
        qa, ka, vt, sag, u, scg = _proj_call(
            x, g_pre[l][None, :], w_l[:, :OFF_MG],
            jnp.tile(g_q[l], 2)[None, :], jnp.tile(g_k[l], 2)[None, :])
        lamp = jnp.stack([lam_q1[l], lam_k1[l], lam_q2[l], lam_k2[l]]).astype(jnp.float32)
        attn = functools.partial(_attn_call, cs_tab=cs_tab, qa=qa, ka=ka, vt=vt, sag=sag,
                                 gs=g_subln[l][:, None], lamp=lamp, lam_init=_lambda_init(l))
        norm_bound = math.sqrt(HEAD_DIM) * NORM_SAFETY
        qk_bound = (norm_bound * Q_SCALE * jnp.max(jnp.abs(g_q[l]))) * (norm_bound * jnp.max(jnp.abs(g_k[l])))
        og = lax.cond(qk_bound < FIXED_SHIFT_MAX_QK,
                      lambda: attn(True), lambda: attn(False))
        x = _merge_call(
            x, og, u, scg, g_pre[l][None, :], w_l[:, OFF_MG:],
            w_attn_out[l].astype(bf), w_conv_out[l].astype(bf), w_out[l].astype(bf),
            w_dw[l], b_dw[l][None, :], g_cn[l][None, :], b_cn[l][None, :])
    return x
```

```python
import functools
import math

import numpy as np
import jax
import jax.numpy as jnp
from jax import lax
from jax.experimental import pallas as pl
from jax.experimental.pallas import tpu as pltpu

D_MODEL = 1024
HEAD_DIM = 64
V_DIM = 2 * HEAD_DIM
ATTN_WIDTH = D_MODEL // 2
N_HEADS = ATTN_WIDTH // V_DIM
CONV_WIDTH = D_MODEL // 2
CONV_KERNEL = 31
CONV_PAD = (CONV_KERNEL - 1) // 2
EPS = 1e-6
ALIBI_SLOPES = tuple(2.0 ** (-8.0 * (i + 1) / N_HEADS) for i in range(N_HEADS))
LOG2E = math.log2(math.e)
Q_SCALE = LOG2E / math.sqrt(HEAD_DIM)

OFF_Q = 0
OFF_K = OFF_Q + N_HEADS * 2 * HEAD_DIM
OFF_V = OFF_K + N_HEADS * 2 * HEAD_DIM
OFF_AG = OFF_V + ATTN_WIDTH
OFF_GLU = OFF_AG + ATTN_WIDTH
OFF_CG = OFF_GLU + 2 * CONV_WIDTH
OFF_MG = OFF_CG + CONV_WIDTH
D_IN = OFF_MG + 2 * D_MODEL

LANES = 128
SUBLANES = 8
VMEM_LIMIT_BYTES = 56 * 1024 * 1024

TM_PROJ = 512
TM_MERGE = 512
TQ_FIXED = 1024
TQ_RUNNING = 512
TK = 1024
SM_ROWS = 16
HALO = 16
CONV_ROWS = 32
CONV_SHIFT0 = HALO - CONV_PAD
CONV_SPAN = ((CONV_SHIFT0 + CONV_KERNEL - 1) // SUBLANES) * SUBLANES

POS_SPLIT = 128
MAX_SEQ = POS_SPLIT * 256
N_PIECES = 3
N_FLIP = 4 * N_PIECES
N_AUG = 5 * N_PIECES
NORM_SAFETY = 1.01
FIXED_SHIFT_MAX_QK = 45.0
NEG_INIT = -1e30

assert TK % TQ_FIXED == 0 and TK % TQ_RUNNING == 0 and HALO >= CONV_PAD and N_AUG <= HEAD_DIM
assert TM_MERGE % CONV_ROWS == 0 and CONV_ROWS % SUBLANES == 0


def _bf16_pieces(v, n=N_PIECES):
    out = []
    r = v
    for _ in range(n):
        p = r.astype(jnp.bfloat16).astype(jnp.float32)
        out.append(p)
        r = r - p
    return out


def _bf16_pieces_py(v, n=N_PIECES):
    out = []
    r = float(v)
    for _ in range(n):
        p = float(np.asarray(r, np.float32).astype(jnp.bfloat16).astype(np.float32))
        out.append(p)
        r = r - p
    return out


def _silu(x):
    return x * jax.nn.sigmoid(x)


def _proj_kernel(x_ref, gpre_ref, w_ref, gq_ref, gk_ref,
                 qa_ref, ka_ref, vt_ref, sag_ref, u_ref, scg_ref):
    tm = x_ref.shape[1]
    x = x_ref[0]
    h = x * lax.rsqrt(jnp.mean(x * x, axis=-1, keepdims=True) + EPS) * gpre_ref[...]
    h = h.astype(jnp.bfloat16)

    def proj(lo, width):
        return jnp.dot(h, w_ref[:, lo:lo + width], preferred_element_type=jnp.float32)

    lane = lax.broadcasted_iota(jnp.int32, (tm, LANES), 1)
    row = pl.program_id(1) * tm + lax.broadcasted_iota(jnp.int32, (tm, LANES), 0)
    lo_half = lane < HEAD_DIM
    aug_idx = lane & (HEAD_DIM - 1)
    grp = [(aug_idx >= N_PIECES * g) & (aug_idx < N_PIECES * (g + 1)) for g in range(5)]
    pos_bits = POS_SPLIT.bit_length() - 1
    pos_hi = (row >> pos_bits).astype(jnp.float32)
    pos_lo = (row & (POS_SPLIT - 1)).astype(jnp.float32)

    norm_bound = math.sqrt(HEAD_DIM) * NORM_SAFETY
    qk_bound = ((norm_bound * Q_SCALE) * jnp.max(jnp.abs(gq_ref[...]), axis=-1, keepdims=True)
                * (norm_bound * jnp.max(jnp.abs(gk_ref[...]), axis=-1, keepdims=True)))

    def half_sums(v):
        return (jnp.sum(jnp.where(lo_half, v, 0.0), axis=-1, keepdims=True),
                jnp.sum(jnp.where(lo_half, 0.0, v), axis=-1, keepdims=True))

    def qk_norm(z, g):
        s_lo, s_hi = half_sums(z * z)
        r = jnp.where(lo_half, lax.rsqrt(s_lo * (1.0 / HEAD_DIM) + EPS),
                      lax.rsqrt(s_hi * (1.0 / HEAD_DIM) + EPS))
        return z * r * g

    row_idx = aug_idx[0:1]

    def put(base, g, vals):
        out = base
        for i, v in enumerate(vals):
            out = jnp.where(row_idx == N_PIECES * g + i, v, out)
        return out

    zero_row = jnp.zeros((1, LANES), jnp.float32)
    bound_row = put(zero_row, 4, _bf16_pieces(zero_row - qk_bound))
    ones_row = jnp.where((row_idx >= N_PIECES * 4) & (row_idx < N_AUG), 1.0, zero_row)
    zero = jnp.zeros((tm, LANES), jnp.float32)
    pos_q = jnp.where(grp[2], pos_hi, jnp.where(grp[3], pos_lo, zero))
    pos_k = jnp.where(grp[0], pos_hi, jnp.where(grp[1], pos_lo, zero))

    zq = proj(OFF_Q, N_HEADS * 2 * HEAD_DIM)
    zk = proj(OFF_K, N_HEADS * 2 * HEAD_DIM)
    for hh in range(N_HEADS):
        c = _bf16_pieces_py(ALIBI_SLOPES[hh] * LOG2E)
        sl = slice(hh * LANES, (hh + 1) * LANES)
        qn = qk_norm(zq[:, sl], gq_ref[...]) * Q_SCALE
        kn = qk_norm(zk[:, sl], gk_ref[...])
        aug_q = pos_q + put(put(bound_row, 0, [POS_SPLIT * v for v in c]), 1, c)
        aug_k = pos_k + put(put(ones_row, 2, [-POS_SPLIT * v for v in c]), 3, [-v for v in c])

        qa_ref[0, 2 * hh] = jnp.where(lo_half, qn, aug_q).astype(jnp.bfloat16)
        qa_ref[0, 2 * hh + 1] = jnp.where(lo_half, aug_q, qn).astype(jnp.bfloat16)
        ka_ref[0, 2 * hh] = jnp.where(lo_half, kn, aug_k).astype(jnp.bfloat16)
        ka_ref[0, 2 * hh + 1] = jnp.where(lo_half, aug_k, kn).astype(jnp.bfloat16)

    zv = proj(OFF_V, ATTN_WIDTH)
    for hh in range(N_HEADS):
        vt_ref[0, hh] = zv[:, hh * V_DIM:(hh + 1) * V_DIM].T.astype(jnp.bfloat16)

    sag_ref[0] = _silu(proj(OFF_AG, ATTN_WIDTH))
    zg = proj(OFF_GLU, 2 * CONV_WIDTH)
    u_ref[0] = zg[:, :CONV_WIDTH] * jax.nn.sigmoid(zg[:, CONV_WIDTH:])
    scg_ref[0] = _silu(proj(OFF_CG, CONV_WIDTH))


def _proj_call(x, g_pre, w_bf, g_q2, g_k2):
    B, S, _ = x.shape
    grid = (B, S // TM_PROJ)
    row_blk = lambda b, i: (b, i, 0)
    const2 = lambda b, i: (0, 0)
    out_shape = (
        jax.ShapeDtypeStruct((B, 2 * N_HEADS, S, LANES), jnp.bfloat16),
        jax.ShapeDtypeStruct((B, 2 * N_HEADS, S, LANES), jnp.bfloat16),
        jax.ShapeDtypeStruct((B, N_HEADS, V_DIM, S), jnp.bfloat16),
        jax.ShapeDtypeStruct((B, S, ATTN_WIDTH), jnp.float32),
        jax.ShapeDtypeStruct((B, S, CONV_WIDTH), jnp.float32),
        jax.ShapeDtypeStruct((B, S, CONV_WIDTH), jnp.float32),
    )
    return pl.pallas_call(
        _proj_kernel,
        grid=grid,
        in_specs=[
            pl.BlockSpec((1, TM_PROJ, D_MODEL), row_blk),
            pl.BlockSpec((1, D_MODEL), const2),
            pl.BlockSpec((D_MODEL, OFF_MG), const2),
            pl.BlockSpec((1, LANES), const2),
            pl.BlockSpec((1, LANES), const2),
        ],
        out_specs=(
            pl.BlockSpec((1, 2 * N_HEADS, TM_PROJ, LANES), lambda b, i: (b, 0, i, 0)),
            pl.BlockSpec((1, 2 * N_HEADS, TM_PROJ, LANES), lambda b, i: (b, 0, i, 0)),
            pl.BlockSpec((1, N_HEADS, V_DIM, TM_PROJ), lambda b, i: (b, 0, 0, i)),
            pl.BlockSpec((1, TM_PROJ, ATTN_WIDTH), row_blk),
            pl.BlockSpec((1, TM_PROJ, CONV_WIDTH), row_blk),
            pl.BlockSpec((1, TM_PROJ, CONV_WIDTH), row_blk),
        ),
        out_shape=out_shape,
        compiler_params=pltpu.CompilerParams(
            dimension_semantics=("arbitrary", "arbitrary"),
            vmem_limit_bytes=VMEM_LIMIT_BYTES),
        name="proj",
    )(x, g_pre, w_bf, g_q2, g_k2)


def _attn_prologue(q1_ref, q2_ref, qs_ref, relu_ref):
    tq = q1_ref.shape[0]
    lane = lax.broadcasted_iota(jnp.int32, (tq, LANES), 1)
    flip_cols = (lane & (HEAD_DIM - 1)) < N_FLIP
    q1 = q1_ref[...]
    q2 = q2_ref[...]
    qs_ref[0, 0] = q1
    qs_ref[0, 1] = q2
    qs_ref[1, 0] = jnp.where(flip_cols & (lane >= HEAD_DIM), -q1, q1)
    qs_ref[1, 1] = jnp.where(flip_cols & (lane < HEAD_DIM), -q2, q2)

    @pl.when(pl.program_id(2) == 0)
    def _():
        d = (lax.broadcasted_iota(jnp.int32, (TK, tq), 0)
             - lax.broadcasted_iota(jnp.int32, (TK, tq), 1))
        for o in range(TK // tq):
            relu_ref[o] = jnp.maximum(d - o * tq, 0).astype(jnp.float32)


def _attn_epilogue(acc_ref, l, sag_ref, gs_ref, lamp_ref, o_ref, lam_init):
    lp = lamp_ref[...]
    lam = (jnp.exp(jnp.sum(lp[0:1] * lp[1:2], axis=-1, keepdims=True))
           - jnp.exp(jnp.sum(lp[2:3] * lp[3:4], axis=-1, keepdims=True)) + lam_init)
    o = acc_ref[0] / l[0] - lam * (acc_ref[1] / l[1])
    o = o * lax.rsqrt(jnp.mean(o * o, axis=0, keepdims=True) + EPS) * gs_ref[...]
    o = o * (1.0 - lam_init)
    o_ref[0] = (o.T * sag_ref[0]).astype(jnp.bfloat16)


def _key_start(t):
    return t * TK if isinstance(t, int) else pl.multiple_of(t * TK, TK)


_NT = (((1,), (1,)), ((), ()))


def _attn_kernel(cs_ref, q1_ref, q2_ref, k1_ref, k2_ref, vt_ref, sag_ref, gs_ref, lamp_ref,
                 o_ref, qs_ref, relu_ref, sa_ref, sb_ref, pa_ref, pb_ref, mla_ref, mlb_ref, acc_ref,
                 *, lam_init):
    k_refs = (k1_ref, k2_ref)
    s_refs = (sa_ref, sb_ref)
    p_refs = (pa_ref, pb_ref)
    ml_refs = (mla_ref, mlb_ref)
    n_kb = k1_ref.shape[0] // TK
    assert n_kb >= 2 and n_kb % 2 == 0
    tq = q1_ref.shape[0]
    cs = cs_ref[pl.program_id(1)]
    i0 = pl.program_id(2) * tq
    kd = i0 // TK
    off_idx = (i0 - kd * TK) // tq

    _attn_prologue(q1_ref, q2_ref, qs_ref, relu_ref)
    acc_ref[...] = jnp.zeros_like(acc_ref)

    def qk(t, slot):
        side = jnp.where(t > kd, 1, 0)
        j0 = _key_start(t)
        for c in range(2):
            s = lax.dot_general(k_refs[c][pl.ds(j0, TK), :], qs_ref[side, c], _NT,
                                preferred_element_type=jnp.float32)
            s_refs[slot][c] = s
            ml_refs[slot][c] = jnp.max(s, axis=0, keepdims=True)

    def fix_diag(t, slot):
        @pl.when(t == kd)
        def _():
            corr = (2.0 * cs) * relu_ref[off_idx]
            for c in range(2):
                s = s_refs[slot][c] - corr
                s_refs[slot][c] = s
                ml_refs[slot][c] = jnp.max(s, axis=0, keepdims=True)

    def sm(slot, m, l):
        m_out, l_out, a_out = [], [], []
        for c in range(2):
            m_new = jnp.maximum(m[c], ml_refs[slot][c])
            alpha = jnp.exp2(m[c] - m_new)
            lsum = jnp.zeros((SUBLANES, tq), jnp.float32)
            for r in range(0, TK, SM_ROWS):
                pc = jnp.exp2(s_refs[slot][c, r:r + SM_ROWS, :] - m_new)
                for r8 in range(0, SM_ROWS, SUBLANES):
                    lsum = lsum + pc[r8:r8 + SUBLANES]
                p_refs[slot][c, r:r + SM_ROWS, :] = pc.astype(jnp.bfloat16)
            l_out.append(alpha * l[c] + jnp.sum(lsum, axis=0, keepdims=True))
            m_out.append(m_new)
            a_out.append(alpha)
        return tuple(m_out), tuple(l_out), tuple(a_out)

    def av(t, slot, alpha):
        vt = vt_ref[:, pl.ds(_key_start(t), TK)]
        for c in range(2):
            acc_ref[c] = alpha[c] * acc_ref[c] + jnp.dot(
                vt, p_refs[slot][c], preferred_element_type=jnp.float32)

    neg = jnp.full((1, tq), NEG_INIT, jnp.float32)
    zero = jnp.zeros((1, tq), jnp.float32)
    m, l = (neg, neg), (zero, zero)

    qk(0, 0)
    fix_diag(0, 0)
    qk(1, 1)
    m, l, alpha = sm(0, m, l)

    def half(t, slot, m, l, alpha):
        fix_diag(t - 1, 1 - slot)
        qk(t, slot)
        m, l, alpha_new = sm(1 - slot, m, l)
        av(t - 2, slot, alpha)
        return m, l, alpha_new

    def pair(g, carry):
        t = 2 + 2 * g
        carry = half(t, 0, *carry)
        return half(t + 1, 1, *carry)

    m, l, alpha = lax.fori_loop(0, (n_kb - 2) // 2, pair, (m, l, alpha))

    fix_diag(n_kb - 1, 1)
    m, l, alpha_last = sm(1, m, l)
    av(n_kb - 2, 0, alpha)
    av(n_kb - 1, 1, alpha_last)
    _attn_epilogue(acc_ref, l, sag_ref, gs_ref, lamp_ref, o_ref, lam_init)


def _attn_fixed_kernel(cs_ref, q1_ref, q2_ref, k1_ref, k2_ref, vt_ref, sag_ref, gs_ref, lamp_ref,
                       o_ref, qs_ref, relu_ref, pa_ref, pb_ref, acc_ref, *, lam_init):
    k_refs = (k1_ref, k2_ref)
    p_refs = (pa_ref, pb_ref)
    n_kb = k1_ref.shape[0] // TK
    assert n_kb >= 2 and n_kb % 2 == 0
    tq = q1_ref.shape[0]
    cs = cs_ref[pl.program_id(1)]
    i0 = pl.program_id(2) * tq
    kd = i0 // TK
    off_idx = (i0 - kd * TK) // tq

    _attn_prologue(q1_ref, q2_ref, qs_ref, relu_ref)
    acc_ref[...] = jnp.zeros_like(acc_ref)

    def stage_a(t, slot, lsum, diag):
        side = jnp.where(t > kd, 1, 0)
        j0 = _key_start(t)
        if diag:
            corr = (2.0 * cs) * relu_ref[off_idx]
        out = []
        for c in range(2):
            s = lax.dot_general(k_refs[c][pl.ds(j0, TK), :], qs_ref[side, c], _NT,
                                preferred_element_type=jnp.float32)
            if diag:
                s = s - corr
            ls = lsum[c]
            for r in range(0, TK, SM_ROWS):
                pc = jnp.exp2(s[r:r + SM_ROWS, :])
                for r8 in range(0, SM_ROWS, SUBLANES):
                    ls = ls + pc[r8:r8 + SUBLANES]
                p_refs[slot][c, r:r + SM_ROWS, :] = pc.astype(jnp.bfloat16)
            out.append(ls)
        return tuple(out)

    def stage_b(t, slot):
        vt = vt_ref[:, pl.ds(_key_start(t), TK)]
        for c in range(2):
            acc_ref[c] += jnp.dot(vt, p_refs[slot][c], preferred_element_type=jnp.float32)

    def nth(n):
        return (n - 1) + jnp.where(n - 1 >= kd, 1, 0)

    zero = jnp.zeros((SUBLANES, tq), jnp.float32)
    lsum = stage_a(kd, 0, (zero, zero), True)
    lsum = stage_a(nth(1), 1, lsum, False)
    stage_b(kd, 0)

    def pair(g, lsum):
        n = 2 + 2 * g
        lsum = stage_a(nth(n), 0, lsum, False)
        stage_b(nth(n - 1), 1)
        lsum = stage_a(nth(n + 1), 1, lsum, False)
        stage_b(nth(n), 0)
        return lsum

    lsum = lax.fori_loop(0, (n_kb - 2) // 2, pair, lsum)
    stage_b(nth(n_kb - 1), 1)
    l = [jnp.sum(ls, axis=0, keepdims=True) for ls in lsum]
    _attn_epilogue(acc_ref, l, sag_ref, gs_ref, lamp_ref, o_ref, lam_init)


def _attn_call(fixed_shift, cs_tab, qa, ka, vt, sag, gs, lamp, lam_init):
    B, _, S, _ = qa.shape
    tq = TQ_FIXED if fixed_shift else TQ_RUNNING
    grid = (B, N_HEADS, S // tq)
    common = [
        pltpu.VMEM((2, 2, tq, LANES), jnp.bfloat16),
        pltpu.VMEM((TK // tq, TK, tq), jnp.float32),
    ]
    tile = lambda dt: pltpu.VMEM((2, TK, tq), dt)
    acc = pltpu.VMEM((2, V_DIM, tq), jnp.float32)
    if fixed_shift:
        body, name = _attn_fixed_kernel, "diff_attn_fixed"
        scratch = common + [tile(jnp.bfloat16), tile(jnp.bfloat16), acc]
    else:
        body, name = _attn_kernel, "diff_attn"
        col_max = pltpu.VMEM((2, 1, tq), jnp.float32)
        scratch = common + [tile(jnp.float32), tile(jnp.float32), tile(jnp.bfloat16),
                            tile(jnp.bfloat16), col_max, col_max, acc]
    return pl.pallas_call(
        functools.partial(body, lam_init=lam_init),
        grid=grid,
        in_specs=[
            pl.BlockSpec(memory_space=pltpu.SMEM),
            pl.BlockSpec((None, None, tq, LANES), lambda b, h, i: (b, 2 * h, i, 0)),
            pl.BlockSpec((None, None, tq, LANES), lambda b, h, i: (b, 2 * h + 1, i, 0)),
            pl.BlockSpec((None, None, S, LANES), lambda b, h, i: (b, 2 * h, 0, 0)),
            pl.BlockSpec((None, None, S, LANES), lambda b, h, i: (b, 2 * h + 1, 0, 0)),
            pl.BlockSpec((None, None, V_DIM, S), lambda b, h, i: (b, h, 0, 0)),
            pl.BlockSpec((1, tq, V_DIM), lambda b, h, i: (b, i, h)),
            pl.BlockSpec((V_DIM, 1), lambda b, h, i: (0, 0)),
            pl.BlockSpec((4, HEAD_DIM), lambda b, h, i: (0, 0)),
        ],
        out_specs=pl.BlockSpec((1, tq, V_DIM), lambda b, h, i: (b, i, h)),
        out_shape=jax.ShapeDtypeStruct((B, S, ATTN_WIDTH), jnp.bfloat16),
        scratch_shapes=scratch,
        compiler_params=pltpu.CompilerParams(
            dimension_semantics=("arbitrary", "arbitrary", "arbitrary"),
            vmem_limit_bytes=VMEM_LIMIT_BYTES),
        name=name,
    )(cs_tab, qa, qa, ka, ka, vt, sag, gs, lamp)


def _merge_kernel(x_ref, og_ref, up_ref, uc_ref, un_ref, scg_ref, gpre_ref, wmg_ref,
                  wao_ref, wco_ref, wout_ref, wdw_ref, bdw_ref, gcn_ref, bcn_ref,
                  o_ref, ucat_ref, ush_ref, ub_ref):
    tm = x_ref.shape[1]
    i = pl.program_id(1)
    n_i = pl.num_programs(1)

    ucat_ref[0:HALO] = jnp.where(i > 0, up_ref[0], 0.0)
    ucat_ref[HALO:HALO + tm] = uc_ref[0]
    ucat_ref[HALO + tm:HALO + tm + HALO] = jnp.where(i < n_i - 1, un_ref[0], 0.0)

    for rho in range(1, SUBLANES):
        ush_ref[rho] = ucat_ref[rho:rho + tm + CONV_SPAN, :]

    wdw = wdw_ref[...]
    for r0 in range(0, tm, CONV_ROWS):
        y = jnp.zeros((CONV_ROWS, CONV_WIDTH), jnp.float32) + bdw_ref[...]
        for t in range(CONV_KERNEL):
            rho = (CONV_SHIFT0 + t) % SUBLANES
            a = r0 + CONV_SHIFT0 + t - rho
            src = ucat_ref[a:a + CONV_ROWS] if rho == 0 else ush_ref[rho, a:a + CONV_ROWS]
            y = y + src * wdw[t:t + 1]
        mu = jnp.mean(y, axis=-1, keepdims=True)
        yc = y - mu
        yn = (yc * lax.rsqrt(jnp.mean(yc * yc, axis=-1, keepdims=True) + EPS) * gcn_ref[...]
              + bcn_ref[...])
        ub_ref[r0:r0 + CONV_ROWS] = (_silu(yn) * scg_ref[0, r0:r0 + CONV_ROWS]).astype(jnp.bfloat16)

    y_a = jnp.dot(og_ref[0], wao_ref[...], preferred_element_type=jnp.float32)
    y_b = jnp.dot(ub_ref[...], wco_ref[...], preferred_element_type=jnp.float32)

    x = x_ref[0]
    h = x * lax.rsqrt(jnp.mean(x * x, axis=-1, keepdims=True) + EPS) * gpre_ref[...]
    zmg = jnp.dot(h.astype(jnp.bfloat16), wmg_ref[...], preferred_element_type=jnp.float32)
    gates = jax.nn.sigmoid(zmg)
    m = gates[:, :D_MODEL] * y_a + gates[:, D_MODEL:] * y_b
    o_ref[0] = x + jnp.dot(m.astype(jnp.bfloat16), wout_ref[...], preferred_element_type=jnp.float32)


def _merge_call(x, og, u, scg, g_pre, w_mg, w_ao, w_co, w_out, w_dw, b_dw, g_cn, b_cn):
    B, S, _ = x.shape
    n_i = S // TM_MERGE
    grid = (B, n_i)
    hpb = TM_MERGE // HALO
    n_hb = S // HALO
    row_blk = lambda b, i: (b, i, 0)
    const2 = lambda b, i: (0, 0)
    return pl.pallas_call(
        _merge_kernel,
        grid=grid,
        in_specs=[
            pl.BlockSpec((1, TM_MERGE, D_MODEL), row_blk),
            pl.BlockSpec((1, TM_MERGE, ATTN_WIDTH), row_blk),
            pl.BlockSpec((1, HALO, CONV_WIDTH), lambda b, i: (b, jnp.maximum(i * hpb - 1, 0), 0)),
            pl.BlockSpec((1, TM_MERGE, CONV_WIDTH), row_blk),
            pl.BlockSpec((1, HALO, CONV_WIDTH), lambda b, i: (b, jnp.minimum((i + 1) * hpb, n_hb - 1), 0)),
            pl.BlockSpec((1, TM_MERGE, CONV_WIDTH), row_blk),
            pl.BlockSpec((1, D_MODEL), const2),
            pl.BlockSpec((D_MODEL, 2 * D_MODEL), const2),
            pl.BlockSpec((ATTN_WIDTH, D_MODEL), const2),
            pl.BlockSpec((CONV_WIDTH, D_MODEL), const2),
            pl.BlockSpec((D_MODEL, D_MODEL), const2),
            pl.BlockSpec((CONV_KERNEL, CONV_WIDTH), const2),
            pl.BlockSpec((1, CONV_WIDTH), const2),
            pl.BlockSpec((1, CONV_WIDTH), const2),
            pl.BlockSpec((1, CONV_WIDTH), const2),
        ],
        out_specs=pl.BlockSpec((1, TM_MERGE, D_MODEL), row_blk),
        out_shape=jax.ShapeDtypeStruct((B, S, D_MODEL), jnp.float32),
        scratch_shapes=[pltpu.VMEM((TM_MERGE + 2 * HALO, CONV_WIDTH), jnp.float32),
                        pltpu.VMEM((SUBLANES, TM_MERGE + CONV_SPAN, CONV_WIDTH), jnp.float32),
                        pltpu.VMEM((TM_MERGE, CONV_WIDTH), jnp.bfloat16)],
        compiler_params=pltpu.CompilerParams(
            dimension_semantics=("arbitrary", "arbitrary"),
            vmem_limit_bytes=VMEM_LIMIT_BYTES),
        name="merge_out",
    )(x, og, u, u, u, scg, g_pre, w_mg, w_ao, w_co, w_out, w_dw, b_dw, g_cn, b_cn)


def _lambda_init(layer_idx):
    return 0.8 - 0.6 * math.exp(-0.3 * layer_idx)


def kernel(x, g_pre, w_in, g_q, g_k, lam_q1, lam_k1, lam_q2, lam_k2, g_subln, w_attn_out, w_dw, b_dw, g_cn, b_cn, w_conv_out, w_out):
    B, S, D = x.shape
    depth = w_in.shape[0]
    assert D == D_MODEL and w_in.shape[2] == D_IN and S <= MAX_SEQ
    assert S % TK == 0 and S % TQ_FIXED == 0 and S % TM_PROJ == 0 and S % TM_MERGE == 0

    bf = jnp.bfloat16
    cs_tab = jnp.asarray([s * LOG2E for s in ALIBI_SLOPES], jnp.float32)
    for l in range(depth):
        w_l = w_in[l].astype(bf)
        qa, ka, vt, sag, u, scg = _proj_call(
            x, g_pre[l][None, :], w_l[:, :OFF_MG],
            jnp.tile(g_q[l], 2)[None, :], jnp.tile(g_k[l], 2)[None, :])
        lamp = jnp.stack([lam_q1[l], lam_k1[l], lam_q2[l], lam_k2[l]]).astype(jnp.float32)
        attn = functools.partial(_attn_call, cs_tab=cs_tab, qa=qa, ka=ka, vt=vt, sag=sag,
                                 gs=g_subln[l][:, None], lamp=lamp, lam_init=_lambda_init(l))
        norm_bound = math.sqrt(HEAD_DIM) * NORM_SAFETY
        qk_bound = (norm_bound * Q_SCALE * jnp.max(jnp.abs(g_q[l]))) * (norm_bound * jnp.max(jnp.abs(g_k[l])))
        og = lax.cond(qk_bound < FIXED_SHIFT_MAX_QK,
                      lambda: attn(True), lambda: attn(False))
        x = _merge_call(
            x, og, u, scg, g_pre[l][None, :], w_l[:, OFF_MG:],
            w_attn_out[l].astype(bf), w_conv_out[l].astype(bf), w_out[l].astype(bf),
            w_dw[l], b_dw[l][None, :], g_cn[l][None, :], b_cn[l][None, :])
    return x
```

```python
import functools
import math

import numpy as np
import jax
import jax.numpy as jnp
from jax import lax
from jax.experimental import pallas as pl
from jax.experimental.pallas import tpu as pltpu

D_MODEL = 1024
HEAD_DIM = 64
V_DIM = 2 * HEAD_DIM
ATTN_WIDTH = D_MODEL // 2
N_HEADS = ATTN_WIDTH // V_DIM
CONV_WIDTH = D_MODEL // 2
CONV_KERNEL = 31
CONV_PAD = (CONV_KERNEL - 1) // 2
EPS = 1e-6
ALIBI_SLOPES = tuple(2.0 ** (-8.0 * (i + 1) / N_HEADS) for i in range(N_HEADS))
LOG2E = math.log2(math.e)
Q_SCALE = LOG2E / math.sqrt(HEAD_DIM)

OFF_Q = 0
OFF_K = OFF_Q + N_HEADS * 2 * HEAD_DIM
OFF_V = OFF_K + N_HEADS * 2 * HEAD_DIM
OFF_AG = OFF_V + ATTN_WIDTH
OFF_GLU = OFF_AG + ATTN_WIDTH
OFF_CG = OFF_GLU + 2 * CONV_WIDTH
OFF_MG = OFF_CG + CONV_WIDTH
D_IN = OFF_MG + 2 * D_MODEL

LANES = 128
SUBLANES = 8
VMEM_LIMIT_BYTES = 56 * 1024 * 1024

TM_PROJ = 512
TM_MERGE = 512
TQ_FIXED = 1024
TQ_RUNNING = 512
TK = 1024
SM_ROWS = 16
HALO = 16
CONV_ROWS = 32
CONV_SHIFT0 = HALO - CONV_PAD
CONV_SPAN = ((CONV_SHIFT0 + CONV_KERNEL - 1) // SUBLANES) * SUBLANES

POS_SPLIT = 128
MAX_SEQ = POS_SPLIT * 256
N_PIECES = 3
N_FLIP = 4 * N_PIECES
N_AUG = 5 * N_PIECES
NORM_SAFETY = 1.01
FIXED_SHIFT_MAX_QK = 45.0
NEG_INIT = -1e30

F8 = jnp.float8_e4m3fn
LO_GAIN = 8.0
F8_NORM_SAFETY = 1.02
QK_ROWS = 256
TINY = 1e-30

assert TK % TQ_FIXED == 0 and TK % TQ_RUNNING == 0 and HALO >= CONV_PAD and N_AUG <= HEAD_DIM
assert TM_MERGE % CONV_ROWS == 0 and CONV_ROWS % SUBLANES == 0


def _bf16_pieces(v, n=N_PIECES):
    out = []
    r = v
    for _ in range(n):
        p = r.astype(jnp.bfloat16).astype(jnp.float32)
        out.append(p)
        r = r - p
    return out


def _bf16_pieces_py(v, n=N_PIECES):
    out = []
    r = float(v)
    for _ in range(n):
        p = float(np.asarray(r, np.float32).astype(jnp.bfloat16).astype(np.float32))
        out.append(p)
        r = r - p
    return out


def _silu(x):
    return x * jax.nn.sigmoid(x)


def _proj_common(x_ref, gpre_ref, w_ref, vt_ref, sag_ref, u_ref, scg_ref):
    tm = x_ref.shape[1]
    x = x_ref[0]
    h = x * lax.rsqrt(jnp.mean(x * x, axis=-1, keepdims=True) + EPS) * gpre_ref[...]
    h = h.astype(jnp.bfloat16)

    def proj(lo, width):
        return jnp.dot(h, w_ref[:, lo:lo + width], preferred_element_type=jnp.float32)

    zv = proj(OFF_V, ATTN_WIDTH)
    for hh in range(N_HEADS):
        vt_ref[0, hh] = zv[:, hh * V_DIM:(hh + 1) * V_DIM].T.astype(jnp.bfloat16)
    sag_ref[0] = _silu(proj(OFF_AG, ATTN_WIDTH))
    zg = proj(OFF_GLU, 2 * CONV_WIDTH)
    u_ref[0] = zg[:, :CONV_WIDTH] * jax.nn.sigmoid(zg[:, CONV_WIDTH:])
    scg_ref[0] = _silu(proj(OFF_CG, CONV_WIDTH))

    lane = lax.broadcasted_iota(jnp.int32, (tm, LANES), 1)
    row = pl.program_id(1) * tm + lax.broadcasted_iota(jnp.int32, (tm, LANES), 0)
    lo_half = lane < HEAD_DIM

    def qk_norm(z, g):
        zz = z * z
        s_lo = jnp.sum(jnp.where(lo_half, zz, 0.0), axis=-1, keepdims=True)
        s_hi = jnp.sum(jnp.where(lo_half, 0.0, zz), axis=-1, keepdims=True)
        r = jnp.where(lo_half, lax.rsqrt(s_lo * (1.0 / HEAD_DIM) + EPS),
                      lax.rsqrt(s_hi * (1.0 / HEAD_DIM) + EPS))
        return z * r * g

    return proj, lane, row, lo_half, qk_norm


def _proj8_kernel(x_ref, gpre_ref, w_ref, gq_ref, gk_ref,
                  qa_ref, ka_ref, vt_ref, sag_ref, u_ref, scg_ref):
    proj, lane, row, lo_half, qk_norm = _proj_common(
        x_ref, gpre_ref, w_ref, vt_ref, sag_ref, u_ref, scg_ref)

    def r8(v):
        return v.astype(F8).astype(jnp.float32)

    def swap_halves(v):
        return pltpu.roll(v, HEAD_DIM, axis=1)

    g_q = jnp.maximum(jnp.max(jnp.abs(gq_ref[...]), axis=-1, keepdims=True), TINY)
    g_k = jnp.maximum(jnp.max(jnp.abs(gk_ref[...]), axis=-1, keepdims=True), TINY)
    a_q = math.sqrt(Q_SCALE) * jnp.sqrt(g_k / g_q)
    a_k = Q_SCALE / a_q

    zq = proj(OFF_Q, N_HEADS * 2 * HEAD_DIM)
    zk = proj(OFF_K, N_HEADS * 2 * HEAD_DIM)
    hi_first = (lo_half, jnp.logical_not(lo_half))
    for hh in range(N_HEADS):
        sl = slice(hh * LANES, (hh + 1) * LANES)
        qs = qk_norm(zq[:, sl], gq_ref[...]) * a_q
        ks = qk_norm(zk[:, sl], gk_ref[...]) * a_k
        qh = r8(qs)
        kh = r8(ks)
        ql = (qs - qh) * LO_GAIN
        kl_s = swap_halves((ks - kh) * LO_GAIN)
        qh_s = swap_halves(qh * (1.0 / LO_GAIN))
        kh_lo = kh * (1.0 / LO_GAIN)
        for c in range(2):
            qa_ref[0, 2 * hh + c, :, 0:LANES] = jnp.where(hi_first[c], qh, qh_s).astype(F8)
            qa_ref[0, 2 * hh + c, :, LANES:2 * LANES] = jnp.where(hi_first[c], ql, 0.0).astype(F8)
            ka_ref[0, 2 * hh + c, :, 0:LANES] = jnp.where(hi_first[c], kh, kl_s).astype(F8)
            ka_ref[0, 2 * hh + c, :, LANES:2 * LANES] = jnp.where(hi_first[c], kh_lo, 0.0).astype(F8)


def _proj_kernel(x_ref, gpre_ref, w_ref, gq_ref, gk_ref,
                 qa_ref, ka_ref, vt_ref, sag_ref, u_ref, scg_ref):
    tm = x_ref.shape[1]
    proj, lane, row, lo_half, qk_norm = _proj_common(
        x_ref, gpre_ref, w_ref, vt_ref, sag_ref, u_ref, scg_ref)
    aug_idx = lane & (HEAD_DIM - 1)
    grp = [(aug_idx >= N_PIECES * g) & (aug_idx < N_PIECES * (g + 1)) for g in range(5)]
    pos_bits = POS_SPLIT.bit_length() - 1
    pos_hi = (row >> pos_bits).astype(jnp.float32)
    pos_lo = (row & (POS_SPLIT - 1)).astype(jnp.float32)

    norm_bound = math.sqrt(HEAD_DIM) * NORM_SAFETY
    qk_bound = ((norm_bound * Q_SCALE) * jnp.max(jnp.abs(gq_ref[...]), axis=-1, keepdims=True)
                * (norm_bound * jnp.max(jnp.abs(gk_ref[...]), axis=-1, keepdims=True)))

    row_idx = aug_idx[0:1]

    def put(base, g, vals):
        out = base
        for i, v in enumerate(vals):
            out = jnp.where(row_idx == N_PIECES * g + i, v, out)
        return out

    zero_row = jnp.zeros((1, LANES), jnp.float32)
    bound_row = put(zero_row, 4, _bf16_pieces(zero_row - qk_bound))
    ones_row = jnp.where((row_idx >= N_PIECES * 4) & (row_idx < N_AUG), 1.0, zero_row)
    zero = jnp.zeros((tm, LANES), jnp.float32)
    pos_q = jnp.where(grp[2], pos_hi, jnp.where(grp[3], pos_lo, zero))
    pos_k = jnp.where(grp[0], pos_hi, jnp.where(grp[1], pos_lo, zero))

    zq = proj(OFF_Q, N_HEADS * 2 * HEAD_DIM)
    zk = proj(OFF_K, N_HEADS * 2 * HEAD_DIM)
    for hh in range(N_HEADS):
        c = _bf16_pieces_py(ALIBI_SLOPES[hh] * LOG2E)
        sl = slice(hh * LANES, (hh + 1) * LANES)
        qn = qk_norm(zq[:, sl], gq_ref[...]) * Q_SCALE
        kn = qk_norm(zk[:, sl], gk_ref[...])
        aug_q = pos_q + put(put(bound_row, 0, [POS_SPLIT * v for v in c]), 1, c)
        aug_k = pos_k + put(put(ones_row, 2, [-POS_SPLIT * v for v in c]), 3, [-v for v in c])

        qa_ref[0, 2 * hh] = jnp.where(lo_half, qn, aug_q).astype(jnp.bfloat16)
        qa_ref[0, 2 * hh + 1] = jnp.where(lo_half, aug_q, qn).astype(jnp.bfloat16)
        ka_ref[0, 2 * hh] = jnp.where(lo_half, kn, aug_k).astype(jnp.bfloat16)
        ka_ref[0, 2 * hh + 1] = jnp.where(lo_half, aug_k, kn).astype(jnp.bfloat16)


def _proj_call(fp8, x, g_pre, w_bf, g_q2, g_k2):
    B, S, _ = x.shape
    grid = (B, S // TM_PROJ)
    row_blk = lambda b, i: (b, i, 0)
    const2 = lambda b, i: (0, 0)
    qk_lanes, qk_dtype = (2 * LANES, F8) if fp8 else (LANES, jnp.bfloat16)
    out_shape = (
        jax.ShapeDtypeStruct((B, 2 * N_HEADS, S, qk_lanes), qk_dtype),
        jax.ShapeDtypeStruct((B, 2 * N_HEADS, S, qk_lanes), qk_dtype),
        jax.ShapeDtypeStruct((B, N_HEADS, V_DIM, S), jnp.bfloat16),
        jax.ShapeDtypeStruct((B, S, ATTN_WIDTH), jnp.float32),
        jax.ShapeDtypeStruct((B, S, CONV_WIDTH), jnp.float32),
        jax.ShapeDtypeStruct((B, S, CONV_WIDTH), jnp.float32),
    )
    return pl.pallas_call(
        _proj8_kernel if fp8 else _proj_kernel,
        grid=grid,
        in_specs=[
            pl.BlockSpec((1, TM_PROJ, D_MODEL), row_blk),
            pl.BlockSpec((1, D_MODEL), const2),
            pl.BlockSpec((D_MODEL, OFF_MG), const2),
            pl.BlockSpec((1, LANES), const2),
            pl.BlockSpec((1, LANES), const2),
        ],
        out_specs=(
            pl.BlockSpec((1, 2 * N_HEADS, TM_PROJ, qk_lanes), lambda b, i: (b, 0, i, 0)),
            pl.BlockSpec((1, 2 * N_HEADS, TM_PROJ, qk_lanes), lambda b, i: (b, 0, i, 0)),
            pl.BlockSpec((1, N_HEADS, V_DIM, TM_PROJ), lambda b, i: (b, 0, 0, i)),
            pl.BlockSpec((1, TM_PROJ, ATTN_WIDTH), row_blk),
            pl.BlockSpec((1, TM_PROJ, CONV_WIDTH), row_blk),
            pl.BlockSpec((1, TM_PROJ, CONV_WIDTH), row_blk),
        ),
        out_shape=out_shape,
        compiler_params=pltpu.CompilerParams(
            dimension_semantics=("arbitrary", "arbitrary"),
            vmem_limit_bytes=VMEM_LIMIT_BYTES),
        name="proj8" if fp8 else "proj",
    )(x, g_pre, w_bf, g_q2, g_k2)


def _attn_prologue(q1_ref, q2_ref, qs_ref, relu_ref):
    tq = q1_ref.shape[0]
    lane = lax.broadcasted_iota(jnp.int32, (tq, LANES), 1)
    flip_cols = (lane & (HEAD_DIM - 1)) < N_FLIP
    q1 = q1_ref[...]
    q2 = q2_ref[...]
    qs_ref[0, 0] = q1
    qs_ref[0, 1] = q2
    qs_ref[1, 0] = jnp.where(flip_cols & (lane >= HEAD_DIM), -q1, q1)
    qs_ref[1, 1] = jnp.where(flip_cols & (lane < HEAD_DIM), -q2, q2)

    @pl.when(pl.program_id(2) == 0)
    def _():
        d = (lax.broadcasted_iota(jnp.int32, (TK, tq), 0)
             - lax.broadcasted_iota(jnp.int32, (TK, tq), 1))
        for o in range(TK // tq):
            relu_ref[o] = jnp.maximum(d - o * tq, 0).astype(jnp.float32)


def _attn_epilogue(acc_ref, l, sag_ref, gs_ref, lamp_ref, o_ref, lam_init):
    lp = lamp_ref[...]
    lam = (jnp.exp(jnp.sum(lp[0:1] * lp[1:2], axis=-1, keepdims=True))
           - jnp.exp(jnp.sum(lp[2:3] * lp[3:4], axis=-1, keepdims=True)) + lam_init)
    o = acc_ref[0] / l[0] - lam * (acc_ref[1] / l[1])
    o = o * lax.rsqrt(jnp.mean(o * o, axis=0, keepdims=True) + EPS) * gs_ref[...]
    o = o * (1.0 - lam_init)
    o_ref[0] = (o.T * sag_ref[0]).astype(jnp.bfloat16)


def _key_start(t):
    return t * TK if isinstance(t, int) else pl.multiple_of(t * TK, TK)


_NT = (((1,), (1,)), ((), ()))


def _attn_kernel(cs_ref, q1_ref, q2_ref, k1_ref, k2_ref, vt_ref, sag_ref, gs_ref, lamp_ref,
                 o_ref, qs_ref, relu_ref, sa_ref, sb_ref, pa_ref, pb_ref, mla_ref, mlb_ref, acc_ref,
                 *, lam_init):
    k_refs = (k1_ref, k2_ref)
    s_refs = (sa_ref, sb_ref)
    p_refs = (pa_ref, pb_ref)
    ml_refs = (mla_ref, mlb_ref)
    n_kb = k1_ref.shape[0] // TK
    assert n_kb >= 2 and n_kb % 2 == 0
    tq = q1_ref.shape[0]
    cs = cs_ref[pl.program_id(1)]
    i0 = pl.program_id(2) * tq
    kd = i0 // TK
    off_idx = (i0 - kd * TK) // tq

    _attn_prologue(q1_ref, q2_ref, qs_ref, relu_ref)
    acc_ref[...] = jnp.zeros_like(acc_ref)

    def qk(t, slot):
        side = jnp.where(t > kd, 1, 0)
        j0 = _key_start(t)
        for c in range(2):
            s = lax.dot_general(k_refs[c][pl.ds(j0, TK), :], qs_ref[side, c], _NT,
                                preferred_element_type=jnp.float32)
            s_refs[slot][c] = s
            ml_refs[slot][c] = jnp.max(s, axis=0, keepdims=True)

    def fix_diag(t, slot):
        @pl.when(t == kd)
        def _():
            corr = (2.0 * cs) * relu_ref[off_idx]
            for c in range(2):
                s = s_refs[slot][c] - corr
                s_refs[slot][c] = s
                ml_refs[slot][c] = jnp.max(s, axis=0, keepdims=True)

    def sm(slot, m, l):
        m_out, l_out, a_out = [], [], []
        for c in range(2):
            m_new = jnp.maximum(m[c], ml_refs[slot][c])
            alpha = jnp.exp2(m[c] - m_new)
            lsum = jnp.zeros((SUBLANES, tq), jnp.float32)
            for r in range(0, TK, SM_ROWS):
                pc = jnp.exp2(s_refs[slot][c, r:r + SM_ROWS, :] - m_new)
                for r8 in range(0, SM_ROWS, SUBLANES):
                    lsum = lsum + pc[r8:r8 + SUBLANES]
                p_refs[slot][c, r:r + SM_ROWS, :] = pc.astype(jnp.bfloat16)
            l_out.append(alpha * l[c] + jnp.sum(lsum, axis=0, keepdims=True))
            m_out.append(m_new)
            a_out.append(alpha)
        return tuple(m_out), tuple(l_out), tuple(a_out)

    def av(t, slot, alpha):
        vt = vt_ref[:, pl.ds(_key_start(t), TK)]
        for c in range(2):
            acc_ref[c] = alpha[c] * acc_ref[c] + jnp.dot(
                vt, p_refs[slot][c], preferred_element_type=jnp.float32)

    neg = jnp.full((1, tq), NEG_INIT, jnp.float32)
    zero = jnp.zeros((1, tq), jnp.float32)
    m, l = (neg, neg), (zero, zero)

    qk(0, 0)
    fix_diag(0, 0)
    qk(1, 1)
    m, l, alpha = sm(0, m, l)

    def half(t, slot, m, l, alpha):
        fix_diag(t - 1, 1 - slot)
        qk(t, slot)
        m, l, alpha_new = sm(1 - slot, m, l)
        av(t - 2, slot, alpha)
        return m, l, alpha_new

    def pair(g, carry):
        t = 2 + 2 * g
        carry = half(t, 0, *carry)
        return half(t + 1, 1, *carry)

    m, l, alpha = lax.fori_loop(0, (n_kb - 2) // 2, pair, (m, l, alpha))

    fix_diag(n_kb - 1, 1)
    m, l, alpha_last = sm(1, m, l)
    av(n_kb - 2, 0, alpha)
    av(n_kb - 1, 1, alpha_last)
    _attn_epilogue(acc_ref, l, sag_ref, gs_ref, lamp_ref, o_ref, lam_init)


def _attn_fixed_kernel(cs_ref, q1_ref, q2_ref, k1_ref, k2_ref, vt_ref, sag_ref, gs_ref, lamp_ref,
                       o_ref, tile_ref, pa_ref, pb_ref, acc_ref, lsum_ref, fin_ref, *, lam_init):
    k_refs = (k1_ref, k2_ref)
    q_refs = (q1_ref, q2_ref)
    p_refs = (pa_ref, pb_ref)
    n_kb = k1_ref.shape[0] // TK
    tq = q1_ref.shape[0]
    assert n_kb >= 2 and n_kb % 2 == 0 and tq == TK
    cs = cs_ref[pl.program_id(1)]
    kd = pl.program_id(2)

    @pl.when(kd == 0)
    def _():
        jr = lax.broadcasted_iota(jnp.int32, (TK, tq), 0)
        ir = lax.broadcasted_iota(jnp.int32, (TK, tq), 1)
        tile_ref[0] = (-cs) * (TK - jr).astype(jnp.float32)
        tile_ref[1] = (-cs) * (jr + 1).astype(jnp.float32)
        tile_ref[2] = (-cs) * jnp.abs(jr - ir).astype(jnp.float32)

    acc_ref[...] = jnp.zeros_like(acc_ref)
    lsum_ref[...] = jnp.zeros_like(lsum_ref)

    def geom(t, diag):
        if diag:
            return 2, None
        grp = jnp.where(t < kd, 0, 1)
        gap = jnp.where(t < kd, kd - t - 1, t - kd - 1).astype(jnp.float32)
        return grp, jnp.exp2(jnp.full((1, 1), -float(TK), jnp.float32) * (cs * gap))

    def stage(ta, slot_a, tb, slot_b, diag_a=False, diag_b=False):
        if ta is not None:
            grp_a, g_a = geom(ta, diag_a)
            j0a = _key_start(ta)
        if tb is not None:
            grp_b, g_b = geom(tb, diag_b)
            j0b = _key_start(tb)
        for c in range(2):
            part = None
            ls = jnp.zeros((SUBLANES, tq), jnp.float32)
            for r in range(0, TK, QK_ROWS):
                if ta is not None:
                    s = lax.dot_general(k_refs[c][pl.ds(pl.multiple_of(j0a + r, QK_ROWS), QK_ROWS), :],
                                        q_refs[c][...], _NT,
                                        preferred_element_type=jnp.float32)
                    for rr in range(0, QK_ROWS, SM_ROWS):
                        pc = jnp.exp2(s[rr:rr + SM_ROWS, :]
                                      + tile_ref[grp_a, r + rr:r + rr + SM_ROWS, :])
                        for r8 in range(0, SM_ROWS, SUBLANES):
                            ls = ls + pc[r8:r8 + SUBLANES]
                        p_refs[slot_a][c, r + rr:r + rr + SM_ROWS, :] = pc.astype(jnp.bfloat16)
                if tb is not None:
                    d = jnp.dot(vt_ref[:, pl.ds(pl.multiple_of(j0b + r, QK_ROWS), QK_ROWS)],
                                p_refs[slot_b][c, r:r + QK_ROWS, :],
                                preferred_element_type=jnp.float32)
                    part = d if part is None else part + d
            if ta is not None:
                lsum_ref[grp_a, c] += ls if g_a is None else g_a * ls
            if tb is not None:
                acc_ref[grp_b, c] += part if g_b is None else g_b * part

    def nth(n):
        return (n - 1) + jnp.where(n - 1 >= kd, 1, 0)

    stage(kd, 0, None, None, diag_a=True)
    stage(nth(1), 1, kd, 0, diag_b=True)

    def pair(g, carry):
        n = 2 + 2 * g
        stage(nth(n), 0, nth(n - 1), 1)
        stage(nth(n + 1), 1, nth(n), 0)
        return carry

    lax.fori_loop(0, (n_kb - 2) // 2, pair, 0)
    stage(None, None, nth(n_kb - 1), 1)

    i_rel = lax.broadcasted_iota(jnp.int32, (1, tq), 1).astype(jnp.float32)
    f_l = jnp.exp2((-cs) * i_rel)
    f_r = jnp.exp2((-cs) * (float(tq - 1) - i_rel))
    l = []
    for c in range(2):
        fin_ref[c] = f_l * acc_ref[0, c] + f_r * acc_ref[1, c] + acc_ref[2, c]
        l.append(jnp.sum(f_l * lsum_ref[0, c] + f_r * lsum_ref[1, c] + lsum_ref[2, c],
                         axis=0, keepdims=True))
    _attn_epilogue(fin_ref, l, sag_ref, gs_ref, lamp_ref, o_ref, lam_init)


def _attn_call(fixed_shift, cs_tab, qa, ka, vt, sag, gs, lamp, lam_init):
    B, _, S, qk_lanes = qa.shape
    tq = TQ_FIXED if fixed_shift else TQ_RUNNING
    grid = (B, N_HEADS, S // tq)
    common = [
        pltpu.VMEM((2, 2, tq, LANES), jnp.bfloat16),
        pltpu.VMEM((TK // tq, TK, tq), jnp.float32),
    ]
    tile = lambda dt: pltpu.VMEM((2, TK, tq), dt)
    acc = pltpu.VMEM((2, V_DIM, tq), jnp.float32)
    if fixed_shift:
        body, name = _attn_fixed_kernel, "diff_attn_fixed"
        scratch = [
            pltpu.VMEM((3, TK, tq), jnp.float32),
            tile(jnp.bfloat16), tile(jnp.bfloat16),
            pltpu.VMEM((3, 2, V_DIM, tq), jnp.float32),
            pltpu.VMEM((3, 2, SUBLANES, tq), jnp.float32),
            acc,
        ]
    else:
        body, name = _attn_kernel, "diff_attn"
        col_max = pltpu.VMEM((2, 1, tq), jnp.float32)
        scratch = common + [tile(jnp.float32), tile(jnp.float32), tile(jnp.bfloat16),
                            tile(jnp.bfloat16), col_max, col_max, acc]
    return pl.pallas_call(
        functools.partial(body, lam_init=lam_init),
        grid=grid,
        in_specs=[
            pl.BlockSpec(memory_space=pltpu.SMEM),
            pl.BlockSpec((None, None, tq, qk_lanes), lambda b, h, i: (b, 2 * h, i, 0)),
            pl.BlockSpec((None, None, tq, qk_lanes), lambda b, h, i: (b, 2 * h + 1, i, 0)),
            pl.BlockSpec((None, None, S, qk_lanes), lambda b, h, i: (b, 2 * h, 0, 0)),
            pl.BlockSpec((None, None, S, qk_lanes), lambda b, h, i: (b, 2 * h + 1, 0, 0)),
            pl.BlockSpec((None, None, V_DIM, S), lambda b, h, i: (b, h, 0, 0)),
            pl.BlockSpec((1, tq, V_DIM), lambda b, h, i: (b, i, h)),
            pl.BlockSpec((V_DIM, 1), lambda b, h, i: (0, 0)),
            pl.BlockSpec((4, HEAD_DIM), lambda b, h, i: (0, 0)),
        ],
        out_specs=pl.BlockSpec((1, tq, V_DIM), lambda b, h, i: (b, i, h)),
        out_shape=jax.ShapeDtypeStruct((B, S, ATTN_WIDTH), jnp.bfloat16),
        scratch_shapes=scratch,
        compiler_params=pltpu.CompilerParams(
            dimension_semantics=("arbitrary", "arbitrary", "arbitrary"),
            vmem_limit_bytes=VMEM_LIMIT_BYTES),
        name=name,
    )(cs_tab, qa, qa, ka, ka, vt, sag, gs, lamp)


def _merge_kernel(x_ref, og_ref, up_ref, uc_ref, un_ref, scg_ref, gpre_ref, wmg_ref,
                  wao_ref, wco_ref, wout_ref, wdw_ref, bdw_ref, gcn_ref, bcn_ref,
                  o_ref, ucat_ref, ush_ref, ub_ref):
    tm = x_ref.shape[1]
    i = pl.program_id(1)
    n_i = pl.num_programs(1)

    ucat_ref[0:HALO] = jnp.where(i > 0, up_ref[0], 0.0)
    ucat_ref[HALO:HALO + tm] = uc_ref[0]
    ucat_ref[HALO + tm:HALO + tm + HALO] = jnp.where(i < n_i - 1, un_ref[0], 0.0)

    for rho in range(1, SUBLANES):
        ush_ref[rho] = ucat_ref[rho:rho + tm + CONV_SPAN, :]

    wdw = wdw_ref[...]
    for r0 in range(0, tm, CONV_ROWS):
        y = jnp.zeros((CONV_ROWS, CONV_WIDTH), jnp.float32) + bdw_ref[...]
        for t in range(CONV_KERNEL):
            rho = (CONV_SHIFT0 + t) % SUBLANES
            a = r0 + CONV_SHIFT0 + t - rho
            src = ucat_ref[a:a + CONV_ROWS] if rho == 0 else ush_ref[rho, a:a + CONV_ROWS]
            y = y + src * wdw[t:t + 1]
        mu = jnp.mean(y, axis=-1, keepdims=True)
        yc = y - mu
        yn = (yc * lax.rsqrt(jnp.mean(yc * yc, axis=-1, keepdims=True) + EPS) * gcn_ref[...]
              + bcn_ref[...])
        ub_ref[r0:r0 + CONV_ROWS] = (_silu(yn) * scg_ref[0, r0:r0 + CONV_ROWS]).astype(jnp.bfloat16)

    y_a = jnp.dot(og_ref[0], wao_ref[...], preferred_element_type=jnp.float32)
    y_b = jnp.dot(ub_ref[...], wco_ref[...], preferred_element_type=jnp.float32)

    x = x_ref[0]
    h = x * lax.rsqrt(jnp.mean(x * x, axis=-1, keepdims=True) + EPS) * gpre_ref[...]
    zmg = jnp.dot(h.astype(jnp.bfloat16), wmg_ref[...], preferred_element_type=jnp.float32)
    gates = jax.nn.sigmoid(zmg)
    m = gates[:, :D_MODEL] * y_a + gates[:, D_MODEL:] * y_b
    o_ref[0] = x + jnp.dot(m.astype(jnp.bfloat16), wout_ref[...], preferred_element_type=jnp.float32)


def _merge_call(x, og, u, scg, g_pre, w_mg, w_ao, w_co, w_out, w_dw, b_dw, g_cn, b_cn):
    B, S, _ = x.shape
    n_i = S // TM_MERGE
    grid = (B, n_i)
    hpb = TM_MERGE // HALO
    n_hb = S // HALO
    row_blk = lambda b, i: (b, i, 0)
    const2 = lambda b, i: (0, 0)
    return pl.pallas_call(
        _merge_kernel,
        grid=grid,
        in_specs=[
            pl.BlockSpec((1, TM_MERGE, D_MODEL), row_blk),
            pl.BlockSpec((1, TM_MERGE, ATTN_WIDTH), row_blk),
            pl.BlockSpec((1, HALO, CONV_WIDTH), lambda b, i: (b, jnp.maximum(i * hpb - 1, 0), 0)),
            pl.BlockSpec((1, TM_MERGE, CONV_WIDTH), row_blk),
            pl.BlockSpec((1, HALO, CONV_WIDTH), lambda b, i: (b, jnp.minimum((i + 1) * hpb, n_hb - 1), 0)),
            pl.BlockSpec((1, TM_MERGE, CONV_WIDTH), row_blk),
            pl.BlockSpec((1, D_MODEL), const2),
            pl.BlockSpec((D_MODEL, 2 * D_MODEL), const2),
            pl.BlockSpec((ATTN_WIDTH, D_MODEL), const2),
            pl.BlockSpec((CONV_WIDTH, D_MODEL), const2),
            pl.BlockSpec((D_MODEL, D_MODEL), const2),
            pl.BlockSpec((CONV_KERNEL, CONV_WIDTH), const2),
            pl.BlockSpec((1, CONV_WIDTH), const2),
            pl.BlockSpec((1, CONV_WIDTH), const2),
            pl.BlockSpec((1, CONV_WIDTH), const2),
        ],
        out_specs=pl.BlockSpec((1, TM_MERGE, D_MODEL), row_blk),
        out_shape=jax.ShapeDtypeStruct((B, S, D_MODEL), jnp.float32),
        scratch_shapes=[pltpu.VMEM((TM_MERGE + 2 * HALO, CONV_WIDTH), jnp.float32),
                        pltpu.VMEM((SUBLANES, TM_MERGE + CONV_SPAN, CONV_WIDTH), jnp.float32),
                        pltpu.VMEM((TM_MERGE, CONV_WIDTH), jnp.bfloat16)],
        compiler_params=pltpu.CompilerParams(
            dimension_semantics=("arbitrary", "arbitrary"),
            vmem_limit_bytes=VMEM_LIMIT_BYTES),
        name="merge_out",
    )(x, og, u, u, u, scg, g_pre, w_mg, w_ao, w_co, w_out, w_dw, b_dw, g_cn, b_cn)


def _lambda_init(layer_idx):
    return 0.8 - 0.6 * math.exp(-0.3 * layer_idx)


def kernel(x, g_pre, w_in, g_q, g_k, lam_q1, lam_k1, lam_q2, lam_k2, g_subln, w_attn_out, w_dw, b_dw, g_cn, b_cn, w_conv_out, w_out):
    B, S, D = x.shape
    depth = w_in.shape[0]
    assert D == D_MODEL and w_in.shape[2] == D_IN and S <= MAX_SEQ
    assert S % TK == 0 and S % TQ_FIXED == 0 and S % TM_PROJ == 0 and S % TM_MERGE == 0

    bf = jnp.bfloat16
    cs_tab = jnp.asarray([s * LOG2E for s in ALIBI_SLOPES], jnp.float32)
    for l in range(depth):
        w_l = w_in[l].astype(bf)
        lamp = jnp.stack([lam_q1[l], lam_k1[l], lam_q2[l], lam_k2[l]]).astype(jnp.float32)

        def front(no_shift, l=l, w_l=w_l, lamp=lamp, x=x):
            qa, ka, vt, sag, u, scg = _proj_call(
                no_shift, x, g_pre[l][None, :], w_l[:, :OFF_MG],
                jnp.tile(g_q[l], 2)[None, :], jnp.tile(g_k[l], 2)[None, :])
            og = _attn_call(no_shift, cs_tab, qa, ka, vt, sag, g_subln[l][:, None], lamp,
                            _lambda_init(l))
            return og, u, scg

        norm_bound = math.sqrt(HEAD_DIM) * F8_NORM_SAFETY
        qk_bound = (norm_bound * Q_SCALE * jnp.max(jnp.abs(g_q[l]))) * (norm_bound * jnp.max(jnp.abs(g_k[l])))
        og, u, scg = lax.cond(qk_bound < FIXED_SHIFT_MAX_QK,
                              lambda: front(True), lambda: front(False))
        x = _merge_call(
            x, og, u, scg, g_pre[l][None, :], w_l[:, OFF_MG:],
            w_attn_out[l].astype(bf), w_conv_out[l].astype(bf), w_out[l].astype(bf),
            w_dw[l], b_dw[l][None, :], g_cn[l][None, :], b_cn[l][None, :])
    return x
```

```python
import functools
import math

import numpy as np
import jax
import jax.numpy as jnp
from jax import lax
from jax.experimental import pallas as pl
from jax.experimental.pallas import tpu as pltpu

D_MODEL = 1024
HEAD_DIM = 64
V_DIM = 2 * HEAD_DIM
ATTN_WIDTH = D_MODEL // 2
N_HEADS = ATTN_WIDTH // V_DIM
CONV_WIDTH = D_MODEL // 2
CONV_KERNEL = 31
CONV_PAD = (CONV_KERNEL - 1) // 2
EPS = 1e-6
ALIBI_SLOPES = tuple(2.0 ** (-8.0 * (i + 1) / N_HEADS) for i in range(N_HEADS))
LOG2E = math.log2(math.e)
Q_SCALE = LOG2E / math.sqrt(HEAD_DIM)

OFF_Q = 0
OFF_K = OFF_Q + N_HEADS * 2 * HEAD_DIM
OFF_V = OFF_K + N_HEADS * 2 * HEAD_DIM
OFF_AG = OFF_V + ATTN_WIDTH
OFF_GLU = OFF_AG + ATTN_WIDTH
OFF_CG = OFF_GLU + 2 * CONV_WIDTH
OFF_MG = OFF_CG + CONV_WIDTH
D_IN = OFF_MG + 2 * D_MODEL

LANES = 128
SUBLANES = 8
VMEM_LIMIT_BYTES = 56 * 1024 * 1024

TM_PROJ = 512
TM_MERGE = 512
TQ_FIXED = 1024
TQ_RUNNING = 512
TK = 1024
SM_ROWS = 16
PAIRS_PER_ITER = 3
HALO = 16
CONV_ROWS = 32
CONV_SHIFT0 = HALO - CONV_PAD
CONV_SPAN = ((CONV_SHIFT0 + CONV_KERNEL - 1) // SUBLANES) * SUBLANES

POS_SPLIT = 128
MAX_SEQ = POS_SPLIT * 256
N_PIECES = 3
N_FLIP = 4 * N_PIECES
N_AUG = 5 * N_PIECES
NORM_SAFETY = 1.01
FIXED_SHIFT_MAX_QK = 45.0
NEG_INIT = -1e30

assert TK % TQ_FIXED == 0 and TK % TQ_RUNNING == 0 and HALO >= CONV_PAD and N_AUG <= HEAD_DIM
assert TM_MERGE % CONV_ROWS == 0 and CONV_ROWS % SUBLANES == 0


def _bf16_pieces(v, n=N_PIECES):
    out = []
    r = v
    for _ in range(n):
        p = r.astype(jnp.bfloat16).astype(jnp.float32)
        out.append(p)
        r = r - p
    return out


def _bf16_pieces_py(v, n=N_PIECES):
    out = []
    r = float(v)
    for _ in range(n):
        p = float(np.asarray(r, np.float32).astype(jnp.bfloat16).astype(np.float32))
        out.append(p)
        r = r - p
    return out


def _silu(x):
    return x * jax.nn.sigmoid(x)


def _proj_kernel(x_ref, gpre_ref, w_ref, gq_ref, gk_ref,
                 qa_ref, ka_ref, vt_ref, sag_ref, u_ref, scg_ref):
    tm = x_ref.shape[1]
    x = x_ref[0]
    h = x * lax.rsqrt(jnp.mean(x * x, axis=-1, keepdims=True) + EPS) * gpre_ref[...]
    h = h.astype(jnp.bfloat16)

    def proj(lo, width):
        return jnp.dot(h, w_ref[:, lo:lo + width], preferred_element_type=jnp.float32)

    lane = lax.broadcasted_iota(jnp.int32, (tm, LANES), 1)
    row = pl.program_id(1) * tm + lax.broadcasted_iota(jnp.int32, (tm, LANES), 0)
    lo_half = lane < HEAD_DIM
    aug_idx = lane & (HEAD_DIM - 1)
    grp = [(aug_idx >= N_PIECES * g) & (aug_idx < N_PIECES * (g + 1)) for g in range(5)]
    pos_bits = POS_SPLIT.bit_length() - 1
    pos_hi = (row >> pos_bits).astype(jnp.float32)
    pos_lo = (row & (POS_SPLIT - 1)).astype(jnp.float32)

    norm_bound = math.sqrt(HEAD_DIM) * NORM_SAFETY
    qk_bound = ((norm_bound * Q_SCALE) * jnp.max(jnp.abs(gq_ref[...]), axis=-1, keepdims=True)
                * (norm_bound * jnp.max(jnp.abs(gk_ref[...]), axis=-1, keepdims=True)))

    def half_sums(v):
        return (jnp.sum(jnp.where(lo_half, v, 0.0), axis=-1, keepdims=True),
                jnp.sum(jnp.where(lo_half, 0.0, v), axis=-1, keepdims=True))

    def qk_norm(z, g):
        s_lo, s_hi = half_sums(z * z)
        r = jnp.where(lo_half, lax.rsqrt(s_lo * (1.0 / HEAD_DIM) + EPS),
                      lax.rsqrt(s_hi * (1.0 / HEAD_DIM) + EPS))
        return z * r * g

    row_idx = aug_idx[0:1]

    def put(base, g, vals):
        out = base
        for i, v in enumerate(vals):
            out = jnp.where(row_idx == N_PIECES * g + i, v, out)
        return out

    zero_row = jnp.zeros((1, LANES), jnp.float32)
    bound_row = put(zero_row, 4, _bf16_pieces(zero_row - qk_bound))
    ones_row = jnp.where((row_idx >= N_PIECES * 4) & (row_idx < N_AUG), 1.0, zero_row)
    zero = jnp.zeros((tm, LANES), jnp.float32)
    pos_q = jnp.where(grp[2], pos_hi, jnp.where(grp[3], pos_lo, zero))
    pos_k = jnp.where(grp[0], pos_hi, jnp.where(grp[1], pos_lo, zero))

    zq = proj(OFF_Q, N_HEADS * 2 * HEAD_DIM)
    zk = proj(OFF_K, N_HEADS * 2 * HEAD_DIM)
    for hh in range(N_HEADS):
        c = _bf16_pieces_py(ALIBI_SLOPES[hh] * LOG2E)
        sl = slice(hh * LANES, (hh + 1) * LANES)
        qn = qk_norm(zq[:, sl], gq_ref[...]) * Q_SCALE
        kn = qk_norm(zk[:, sl], gk_ref[...])
        aug_q = pos_q + put(put(bound_row, 0, [POS_SPLIT * v for v in c]), 1, c)
        aug_k = pos_k + put(put(ones_row, 2, [-POS_SPLIT * v for v in c]), 3, [-v for v in c])

        qa_ref[0, 2 * hh] = jnp.where(lo_half, qn, aug_q).astype(jnp.bfloat16)
        qa_ref[0, 2 * hh + 1] = jnp.where(lo_half, aug_q, qn).astype(jnp.bfloat16)
        ka_ref[0, 2 * hh] = jnp.where(lo_half, kn, aug_k).astype(jnp.bfloat16)
        ka_ref[0, 2 * hh + 1] = jnp.where(lo_half, aug_k, kn).astype(jnp.bfloat16)

    zv = proj(OFF_V, ATTN_WIDTH)
    for hh in range(N_HEADS):
        vt_ref[0, hh] = zv[:, hh * V_DIM:(hh + 1) * V_DIM].T.astype(jnp.bfloat16)

    sag_ref[0] = _silu(proj(OFF_AG, ATTN_WIDTH))
    zg = proj(OFF_GLU, 2 * CONV_WIDTH)
    u_ref[0] = zg[:, :CONV_WIDTH] * jax.nn.sigmoid(zg[:, CONV_WIDTH:])
    scg_ref[0] = _silu(proj(OFF_CG, CONV_WIDTH))


def _proj_call(x, g_pre, w_bf, g_q2, g_k2):
    B, S, _ = x.shape
    grid = (B, S // TM_PROJ)
    row_blk = lambda b, i: (b, i, 0)
    const2 = lambda b, i: (0, 0)
    out_shape = (
        jax.ShapeDtypeStruct((B, 2 * N_HEADS, S, LANES), jnp.bfloat16),
        jax.ShapeDtypeStruct((B, 2 * N_HEADS, S, LANES), jnp.bfloat16),
        jax.ShapeDtypeStruct((B, N_HEADS, V_DIM, S), jnp.bfloat16),
        jax.ShapeDtypeStruct((B, S, ATTN_WIDTH), jnp.float32),
        jax.ShapeDtypeStruct((B, S, CONV_WIDTH), jnp.float32),
        jax.ShapeDtypeStruct((B, S, CONV_WIDTH), jnp.float32),
    )
    return pl.pallas_call(
        _proj_kernel,
        grid=grid,
        in_specs=[
            pl.BlockSpec((1, TM_PROJ, D_MODEL), row_blk),
            pl.BlockSpec((1, D_MODEL), const2),
            pl.BlockSpec((D_MODEL, OFF_MG), const2),
            pl.BlockSpec((1, LANES), const2),
            pl.BlockSpec((1, LANES), const2),
        ],
        out_specs=(
            pl.BlockSpec((1, 2 * N_HEADS, TM_PROJ, LANES), lambda b, i: (b, 0, i, 0)),
            pl.BlockSpec((1, 2 * N_HEADS, TM_PROJ, LANES), lambda b, i: (b, 0, i, 0)),
            pl.BlockSpec((1, N_HEADS, V_DIM, TM_PROJ), lambda b, i: (b, 0, 0, i)),
            pl.BlockSpec((1, TM_PROJ, ATTN_WIDTH), row_blk),
            pl.BlockSpec((1, TM_PROJ, CONV_WIDTH), row_blk),
            pl.BlockSpec((1, TM_PROJ, CONV_WIDTH), row_blk),
        ),
        out_shape=out_shape,
        compiler_params=pltpu.CompilerParams(
            dimension_semantics=("arbitrary", "arbitrary"),
            vmem_limit_bytes=VMEM_LIMIT_BYTES),
        name="proj",
    )(x, g_pre, w_bf, g_q2, g_k2)


def _attn_prologue(q1_ref, q2_ref, qs_ref, relu_ref):
    tq = q1_ref.shape[0]
    lane = lax.broadcasted_iota(jnp.int32, (tq, LANES), 1)
    flip_cols = (lane & (HEAD_DIM - 1)) < N_FLIP
    q1 = q1_ref[...]
    q2 = q2_ref[...]
    qs_ref[0, 0] = q1
    qs_ref[0, 1] = q2
    qs_ref[1, 0] = jnp.where(flip_cols & (lane >= HEAD_DIM), -q1, q1)
    qs_ref[1, 1] = jnp.where(flip_cols & (lane < HEAD_DIM), -q2, q2)

    @pl.when(pl.program_id(2) == 0)
    def _():
        d = (lax.broadcasted_iota(jnp.int32, (TK, tq), 0)
             - lax.broadcasted_iota(jnp.int32, (TK, tq), 1))
        for o in range(TK // tq):
            relu_ref[o] = jnp.maximum(d - o * tq, 0).astype(jnp.float32)


def _attn_epilogue(acc_ref, l, sag_ref, gs_ref, lamp_ref, o_ref, lam_init):
    lp = lamp_ref[...]
    lam = (jnp.exp(jnp.sum(lp[0:1] * lp[1:2], axis=-1, keepdims=True))
           - jnp.exp(jnp.sum(lp[2:3] * lp[3:4], axis=-1, keepdims=True)) + lam_init)
    o = acc_ref[0] / l[0] - lam * (acc_ref[1] / l[1])
    o = o * lax.rsqrt(jnp.mean(o * o, axis=0, keepdims=True) + EPS) * gs_ref[...]
    o = o * (1.0 - lam_init)
    o_ref[0] = (o.T * sag_ref[0]).astype(jnp.bfloat16)


def _key_start(t):
    return t * TK if isinstance(t, int) else pl.multiple_of(t * TK, TK)


_NT = (((1,), (1,)), ((), ()))


def _attn_kernel(cs_ref, q1_ref, q2_ref, k1_ref, k2_ref, vt_ref, sag_ref, gs_ref, lamp_ref,
                 o_ref, qs_ref, relu_ref, sa_ref, sb_ref, pa_ref, pb_ref, mla_ref, mlb_ref, acc_ref,
                 *, lam_init):
    k_refs = (k1_ref, k2_ref)
    s_refs = (sa_ref, sb_ref)
    p_refs = (pa_ref, pb_ref)
    ml_refs = (mla_ref, mlb_ref)
    n_kb = k1_ref.shape[0] // TK
    assert n_kb >= 2 and n_kb % 2 == 0
    tq = q1_ref.shape[0]
    cs = cs_ref[pl.program_id(1)]
    i0 = pl.program_id(2) * tq
    kd = i0 // TK
    off_idx = (i0 - kd * TK) // tq

    _attn_prologue(q1_ref, q2_ref, qs_ref, relu_ref)
    acc_ref[...] = jnp.zeros_like(acc_ref)

    def qk(t, slot):
        side = jnp.where(t > kd, 1, 0)
        j0 = _key_start(t)
        for c in range(2):
            s = lax.dot_general(k_refs[c][pl.ds(j0, TK), :], qs_ref[side, c], _NT,
                                preferred_element_type=jnp.float32)
            s_refs[slot][c] = s
            ml_refs[slot][c] = jnp.max(s, axis=0, keepdims=True)

    def fix_diag(t, slot):
        @pl.when(t == kd)
        def _():
            corr = (2.0 * cs) * relu_ref[off_idx]
            for c in range(2):
                s = s_refs[slot][c] - corr
                s_refs[slot][c] = s
                ml_refs[slot][c] = jnp.max(s, axis=0, keepdims=True)

    def sm(slot, m, l):
        m_out, l_out, a_out = [], [], []
        for c in range(2):
            m_new = jnp.maximum(m[c], ml_refs[slot][c])
            alpha = jnp.exp2(m[c] - m_new)
            lsum = jnp.zeros((SUBLANES, tq), jnp.float32)
            for r in range(0, TK, SM_ROWS):
                pc = jnp.exp2(s_refs[slot][c, r:r + SM_ROWS, :] - m_new)
                for r8 in range(0, SM_ROWS, SUBLANES):
                    lsum = lsum + pc[r8:r8 + SUBLANES]
                p_refs[slot][c, r:r + SM_ROWS, :] = pc.astype(jnp.bfloat16)
            l_out.append(alpha * l[c] + jnp.sum(lsum, axis=0, keepdims=True))
            m_out.append(m_new)
            a_out.append(alpha)
        return tuple(m_out), tuple(l_out), tuple(a_out)

    def av(t, slot, alpha):
        vt = vt_ref[:, pl.ds(_key_start(t), TK)]
        for c in range(2):
            acc_ref[c] = alpha[c] * acc_ref[c] + jnp.dot(
                vt, p_refs[slot][c], preferred_element_type=jnp.float32)

    neg = jnp.full((1, tq), NEG_INIT, jnp.float32)
    zero = jnp.zeros((1, tq), jnp.float32)
    m, l = (neg, neg), (zero, zero)

    qk(0, 0)
    fix_diag(0, 0)
    qk(1, 1)
    m, l, alpha = sm(0, m, l)

    def half(t, slot, m, l, alpha):
        fix_diag(t - 1, 1 - slot)
        qk(t, slot)
        m, l, alpha_new = sm(1 - slot, m, l)
        av(t - 2, slot, alpha)
        return m, l, alpha_new

    def pair(g, carry):
        t = 2 + 2 * g
        carry = half(t, 0, *carry)
        return half(t + 1, 1, *carry)

    m, l, alpha = lax.fori_loop(0, (n_kb - 2) // 2, pair, (m, l, alpha))

    fix_diag(n_kb - 1, 1)
    m, l, alpha_last = sm(1, m, l)
    av(n_kb - 2, 0, alpha)
    av(n_kb - 1, 1, alpha_last)
    _attn_epilogue(acc_ref, l, sag_ref, gs_ref, lamp_ref, o_ref, lam_init)


def _attn_fixed_kernel(cs_ref, q1_ref, q2_ref, k1_ref, k2_ref, vt_ref, sag_ref, gs_ref, lamp_ref,
                       o_ref, qs_ref, relu_ref, pa_ref, pb_ref, acc_ref, *, lam_init):
    k_refs = (k1_ref, k2_ref)
    p_refs = (pa_ref, pb_ref)
    n_kb = k1_ref.shape[0] // TK
    assert n_kb >= 2 and n_kb % 2 == 0
    tq = q1_ref.shape[0]
    cs = cs_ref[pl.program_id(1)]
    i0 = pl.program_id(2) * tq
    kd = i0 // TK
    off_idx = (i0 - kd * TK) // tq

    _attn_prologue(q1_ref, q2_ref, qs_ref, relu_ref)
    acc_ref[...] = jnp.zeros_like(acc_ref)

    def stage_a(t, slot, lsum, diag):
        side = jnp.where(t > kd, 1, 0)
        j0 = _key_start(t)
        if diag:
            corr = (2.0 * cs) * relu_ref[off_idx]
        out = []
        for c in range(2):
            s = lax.dot_general(k_refs[c][pl.ds(j0, TK), :], qs_ref[side, c], _NT,
                                preferred_element_type=jnp.float32)
            if diag:
                s = s - corr
            ls = lsum[c]
            for r in range(0, TK, SM_ROWS):
                pc = jnp.exp2(s[r:r + SM_ROWS, :])
                for r8 in range(0, SM_ROWS, SUBLANES):
                    ls = ls + pc[r8:r8 + SUBLANES]
                p_refs[slot][c, r:r + SM_ROWS, :] = pc.astype(jnp.bfloat16)
            out.append(ls)
        return tuple(out)

    def stage_b(t, slot):
        vt = vt_ref[:, pl.ds(_key_start(t), TK)]
        for c in range(2):
            acc_ref[c] += jnp.dot(vt, p_refs[slot][c], preferred_element_type=jnp.float32)

    def nth(n):
        return (n - 1) + jnp.where(n - 1 >= kd, 1, 0)

    zero = jnp.zeros((SUBLANES, tq), jnp.float32)
    lsum = stage_a(kd, 0, (zero, zero), True)
    lsum = stage_a(nth(1), 1, lsum, False)
    stage_b(kd, 0)

    def pair(n, lsum):
        lsum = stage_a(nth(n), 0, lsum, False)
        stage_b(nth(n - 1), 1)
        lsum = stage_a(nth(n + 1), 1, lsum, False)
        stage_b(nth(n), 0)
        return lsum

    n_pairs = (n_kb - 2) // 2
    n0 = 2
    for _ in range(n_pairs % PAIRS_PER_ITER):
        lsum = pair(n0, lsum)
        n0 += 2

    def body(g, lsum):
        n = n0 + 2 * PAIRS_PER_ITER * g
        for k in range(PAIRS_PER_ITER):
            lsum = pair(n + 2 * k, lsum)
        return lsum

    lsum = lax.fori_loop(0, n_pairs // PAIRS_PER_ITER, body, lsum)
    stage_b(nth(n_kb - 1), 1)
    l = [jnp.sum(ls, axis=0, keepdims=True) for ls in lsum]
    _attn_epilogue(acc_ref, l, sag_ref, gs_ref, lamp_ref, o_ref, lam_init)


def _attn_call(fixed_shift, cs_tab, qa, ka, vt, sag, gs, lamp, lam_init):
    B, _, S, _ = qa.shape
    tq = TQ_FIXED if fixed_shift else TQ_RUNNING
    grid = (B, N_HEADS, S // tq)
    common = [
        pltpu.VMEM((2, 2, tq, LANES), jnp.bfloat16),
        pltpu.VMEM((TK // tq, TK, tq), jnp.float32),
    ]
    tile = lambda dt: pltpu.VMEM((2, TK, tq), dt)
    acc = pltpu.VMEM((2, V_DIM, tq), jnp.float32)
    if fixed_shift:
        body, name = _attn_fixed_kernel, "diff_attn_fixed"
        scratch = common + [tile(jnp.bfloat16), tile(jnp.bfloat16), acc]
    else:
        body, name = _attn_kernel, "diff_attn"
        col_max = pltpu.VMEM((2, 1, tq), jnp.float32)
        scratch = common + [tile(jnp.float32), tile(jnp.float32), tile(jnp.bfloat16),
                            tile(jnp.bfloat16), col_max, col_max, acc]
    return pl.pallas_call(
        functools.partial(body, lam_init=lam_init),
        grid=grid,
        in_specs=[
            pl.BlockSpec(memory_space=pltpu.SMEM),
            pl.BlockSpec((None, None, tq, LANES), lambda b, h, i: (b, 2 * h, i, 0)),
            pl.BlockSpec((None, None, tq, LANES), lambda b, h, i: (b, 2 * h + 1, i, 0)),
            pl.BlockSpec((None, None, S, LANES), lambda b, h, i: (b, 2 * h, 0, 0)),
            pl.BlockSpec((None, None, S, LANES), lambda b, h, i: (b, 2 * h + 1, 0, 0)),
            pl.BlockSpec((None, None, V_DIM, S), lambda b, h, i: (b, h, 0, 0)),
            pl.BlockSpec((1, tq, V_DIM), lambda b, h, i: (b, i, h)),
            pl.BlockSpec((V_DIM, 1), lambda b, h, i: (0, 0)),
            pl.BlockSpec((4, HEAD_DIM), lambda b, h, i: (0, 0)),
        ],
        out_specs=pl.BlockSpec((1, tq, V_DIM), lambda b, h, i: (b, i, h)),
        out_shape=jax.ShapeDtypeStruct((B, S, ATTN_WIDTH), jnp.bfloat16),
        scratch_shapes=scratch,
        compiler_params=pltpu.CompilerParams(
            dimension_semantics=("arbitrary", "arbitrary", "arbitrary"),
            vmem_limit_bytes=VMEM_LIMIT_BYTES),
        name=name,
    )(cs_tab, qa, qa, ka, ka, vt, sag, gs, lamp)


def _merge_kernel(x_ref, og_ref, up_ref, uc_ref, un_ref, scg_ref, gpre_ref, wmg_ref,
                  wao_ref, wco_ref, wout_ref, wdw_ref, bdw_ref, gcn_ref, bcn_ref,
                  o_ref, ucat_ref, ush_ref, ub_ref):
    tm = x_ref.shape[1]
    i = pl.program_id(1)
    n_i = pl.num_programs(1)

    ucat_ref[0:HALO] = jnp.where(i > 0, up_ref[0], 0.0)
    ucat_ref[HALO:HALO + tm] = uc_ref[0]
    ucat_ref[HALO + tm:HALO + tm + HALO] = jnp.where(i < n_i - 1, un_ref[0], 0.0)

    for rho in range(1, SUBLANES):
        ush_ref[rho] = ucat_ref[rho:rho + tm + CONV_SPAN, :]

    wdw = wdw_ref[...]
    for r0 in range(0, tm, CONV_ROWS):
        y = jnp.zeros((CONV_ROWS, CONV_WIDTH), jnp.float32) + bdw_ref[...]
        for t in range(CONV_KERNEL):
            rho = (CONV_SHIFT0 + t) % SUBLANES
            a = r0 + CONV_SHIFT0 + t - rho
            src = ucat_ref[a:a + CONV_ROWS] if rho == 0 else ush_ref[rho, a:a + CONV_ROWS]
            y = y + src * wdw[t:t + 1]
        mu = jnp.mean(y, axis=-1, keepdims=True)
        yc = y - mu
        yn = (yc * lax.rsqrt(jnp.mean(yc * yc, axis=-1, keepdims=True) + EPS) * gcn_ref[...]
              + bcn_ref[...])
        ub_ref[r0:r0 + CONV_ROWS] = (_silu(yn) * scg_ref[0, r0:r0 + CONV_ROWS]).astype(jnp.bfloat16)

    y_a = jnp.dot(og_ref[0], wao_ref[...], preferred_element_type=jnp.float32)
    y_b = jnp.dot(ub_ref[...], wco_ref[...], preferred_element_type=jnp.float32)

    x = x_ref[0]
    h = x * lax.rsqrt(jnp.mean(x * x, axis=-1, keepdims=True) + EPS) * gpre_ref[...]
    zmg = jnp.dot(h.astype(jnp.bfloat16), wmg_ref[...], preferred_element_type=jnp.float32)
    gates = jax.nn.sigmoid(zmg)
    m = gates[:, :D_MODEL] * y_a + gates[:, D_MODEL:] * y_b
    o_ref[0] = x + jnp.dot(m.astype(jnp.bfloat16), wout_ref[...], preferred_element_type=jnp.float32)


def _merge_call(x, og, u, scg, g_pre, w_mg, w_ao, w_co, w_out, w_dw, b_dw, g_cn, b_cn):
    B, S, _ = x.shape
    n_i = S // TM_MERGE
    grid = (B, n_i)
    hpb = TM_MERGE // HALO
    n_hb = S // HALO
    row_blk = lambda b, i: (b, i, 0)
    const2 = lambda b, i: (0, 0)
    return pl.pallas_call(
        _merge_kernel,
        grid=grid,
        in_specs=[
            pl.BlockSpec((1, TM_MERGE, D_MODEL), row_blk),
            pl.BlockSpec((1, TM_MERGE, ATTN_WIDTH), row_blk),
            pl.BlockSpec((1, HALO, CONV_WIDTH), lambda b, i: (b, jnp.maximum(i * hpb - 1, 0), 0)),
            pl.BlockSpec((1, TM_MERGE, CONV_WIDTH), row_blk),
            pl.BlockSpec((1, HALO, CONV_WIDTH), lambda b, i: (b, jnp.minimum((i + 1) * hpb, n_hb - 1), 0)),
            pl.BlockSpec((1, TM_MERGE, CONV_WIDTH), row_blk),
            pl.BlockSpec((1, D_MODEL), const2),
            pl.BlockSpec((D_MODEL, 2 * D_MODEL), const2),
            pl.BlockSpec((ATTN_WIDTH, D_MODEL), const2),
            pl.BlockSpec((CONV_WIDTH, D_MODEL), const2),
            pl.BlockSpec((D_MODEL, D_MODEL), const2),
            pl.BlockSpec((CONV_KERNEL, CONV_WIDTH), const2),
            pl.BlockSpec((1, CONV_WIDTH), const2),
            pl.BlockSpec((1, CONV_WIDTH), const2),
            pl.BlockSpec((1, CONV_WIDTH), const2),
        ],
        out_specs=pl.BlockSpec((1, TM_MERGE, D_MODEL), row_blk),
        out_shape=jax.ShapeDtypeStruct((B, S, D_MODEL), jnp.float32),
        scratch_shapes=[pltpu.VMEM((TM_MERGE + 2 * HALO, CONV_WIDTH), jnp.float32),
                        pltpu.VMEM((SUBLANES, TM_MERGE + CONV_SPAN, CONV_WIDTH), jnp.float32),
                        pltpu.VMEM((TM_MERGE, CONV_WIDTH), jnp.bfloat16)],
        compiler_params=pltpu.CompilerParams(
            dimension_semantics=("arbitrary", "arbitrary"),
            vmem_limit_bytes=VMEM_LIMIT_BYTES),
        name="merge_out",
    )(x, og, u, u, u, scg, g_pre, w_mg, w_ao, w_co, w_out, w_dw, b_dw, g_cn, b_cn)


def _lambda_init(layer_idx):
    return 0.8 - 0.6 * math.exp(-0.3 * layer_idx)


def kernel(x, g_pre, w_in, g_q, g_k, lam_q1, lam_k1, lam_q2, lam_k2, g_subln, w_attn_out, w_dw, b_dw, g_cn, b_cn, w_conv_out, w_out):
    B, S, D = x.shape
    depth = w_in.shape[0]
    assert D == D_MODEL and w_in.shape[2] == D_IN and S <= MAX_SEQ
    assert S % TK == 0 and S % TQ_FIXED == 0 and S % TM_PROJ == 0 and S % TM_MERGE == 0

    bf = jnp.bfloat16
    cs_tab = jnp.asarray([s * LOG2E for s in ALIBI_SLOPES], jnp.float32)
    for l in range(depth):
        w_l = w_in[l].astype(bf)
        qa, ka, vt, sag, u, scg = _proj_call(
            x, g_pre[l][None, :], w_l[:, :OFF_MG],
            jnp.tile(g_q[l], 2)[None, :], jnp.tile(g_k[l], 2)[None, :])
        lamp = jnp.stack([lam_q1[l], lam_k1[l], lam_q2[l], lam_k2[l]]).astype(jnp.float32)
        attn = functools.partial(_attn_call, cs_tab=cs_tab, qa=qa, ka=ka, vt=vt, sag=sag,
                                 gs=g_subln[l][:, None], lamp=lamp, lam_init=_lambda_init(l))
        norm_bound = math.sqrt(HEAD_DIM) * NORM_SAFETY
        qk_bound = (norm_bound * Q_SCALE * jnp.max(jnp.abs(g_q[l]))) * (norm_bound * jnp.max(jnp.abs(g_k[l])))
        og = lax.cond(qk_bound < FIXED_SHIFT_MAX_QK,
                      lambda: attn(True), lambda: attn(False))
        x = _merge_call(
            x, og, u, scg, g_pre[l][None, :], w_l[:, OFF_MG:],
            w_attn_out[l].astype(bf), w_conv_out[l].astype(bf), w_out[l].astype(bf),
            w_dw[l], b_dw[l][None, :], g_cn[l][None, :], b_cn[l][None, :])
    return x
```
